```python
import math
import jax
import jax.numpy as jnp
from jax import lax
import numpy as np

D_MODEL = 1024
BATCH = 2
SEQ = 8192
DEPTH = 4
DEC_BATCH = 32
DEC_SEQ = 1
PAST_LEN = 8192
PAGE_SIZE = 128

N_A_LAYERS = DEPTH // 2
N_B_LAYERS = DEPTH - N_A_LAYERS
N_DENSE_LAYERS = (DEPTH + 1) // 2
N_MOE_LAYERS = DEPTH // 2
GROUP_SIZE = 16
N_GROUPS = D_MODEL // GROUP_SIZE
STATE_DIM = 64
SCAN_CHUNK = 128
DT_MIN = 1e-3
DT_MAX = 1e-1
N_HEADS = 16
HEAD_DIM = D_MODEL // N_HEADS
ATTN_BLOCK = 128
ATTN_SCALE = HEAD_DIM ** -0.5
NEG_INF = -1e30
D_FF = ((8 * D_MODEL // 3 + 127) // 128) * 128
N_EXPERTS = 8
TOP_K = 2
ALPHA = (2.0 * DEPTH) ** 0.25
BETA = (8.0 * DEPTH) ** -0.25
LN_EPS = 1e-5

kernel_name = "yoco_s5_fox_deepnorm_adaln_step"


def layer_norm(x, g, b):
    xf = x.astype(jnp.float32)
    mu = jnp.mean(xf, axis=-1, keepdims=True)
    var = jnp.mean(jnp.square(xf - mu), axis=-1, keepdims=True)
    return ((xf - mu) * lax.rsqrt(var + LN_EPS) * g.astype(jnp.float32) + b.astype(jnp.float32)).astype(x.dtype)


def swiglu(h, w_gate, w_up, w_down):
    return (jax.nn.silu(h @ w_gate) * (h @ w_up)) @ w_down


def moe_ffn(h, w_router, b_router, w_gate, w_up, w_down):
    logits = (h @ w_router + b_router).astype(jnp.float32)
    top_val, top_idx = lax.top_k(logits, TOP_K)
    top_w = jax.nn.softmax(top_val, axis=-1)
    gates = jnp.sum(jax.nn.one_hot(top_idx, N_EXPERTS, dtype=jnp.float32) * top_w[..., None], axis=-2)
    out = jnp.zeros(h.shape, jnp.float32)
    for e in range(N_EXPERTS):
        out = out + gates[..., e:e + 1] * swiglu(h, w_gate[e], w_up[e], w_down[e])
    return out.astype(h.dtype)


def _complex_affine_combine(left, right):
    a1r, a1i, b1r, b1i = left
    a2r, a2i, b2r, b2i = right
    ar = a2r * a1r - a2i * a1i
    ai = a2r * a1i + a2i * a1r
    br = a2r * b1r - a2i * b1i + b2r
    bi = a2r * b1i + a2i * b1r + b2i
    return ar, ai, br, bi


def ssm_mixer(u, h0_re, h0_im, lam_re, lam_im, log_dt, b_re, b_im, c_re, c_im, d_skip, w_glu):
    f32 = jnp.float32
    n, t, _ = u.shape
    uf = u.astype(f32)
    dt = jnp.exp(log_dt.astype(f32))[:, None]
    lr = lam_re.astype(f32)
    li = lam_im.astype(f32)
    mag = jnp.exp(lr * dt)
    a_re = mag * jnp.cos(li * dt)
    a_im = mag * jnp.sin(li * dt)
    den = lr * lr + li * li
    nr = a_re - 1.0
    k_re = (nr * lr + a_im * li) / den
    k_im = (a_im * lr - nr * li) / den
    br = b_re.astype(f32)
    bi = b_im.astype(f32)
    bb_re = k_re[..., None] * br - k_im[..., None] * bi
    bb_im = k_re[..., None] * bi + k_im[..., None] * br
    cr = c_re.astype(f32)
    ci = c_im.astype(f32)
    chunk = SCAN_CHUNK if t % SCAN_CHUNK == 0 else t
    n_chunks = t // chunk
    ug = uf.reshape(n, n_chunks, chunk, N_GROUPS, GROUP_SIZE).swapaxes(0, 1)
    a_re_b = jnp.broadcast_to(a_re, (n, chunk, N_GROUPS, STATE_DIM))
    a_im_b = jnp.broadcast_to(a_im, (n, chunk, N_GROUPS, STATE_DIM))

    def step(carry, u_c):
        h_re, h_im = carry
        x_re = jnp.einsum("ntgh,gph->ntgp", u_c, bb_re)
        x_im = jnp.einsum("ntgh,gph->ntgp", u_c, bb_im)
        x_re = x_re.at[:, 0].add(a_re * h_re - a_im * h_im)
        x_im = x_im.at[:, 0].add(a_re * h_im + a_im * h_re)
        _, _, s_re, s_im = lax.associative_scan(_complex_affine_combine, (a_re_b, a_im_b, x_re, x_im), axis=1)
        y = jnp.einsum("ntgp,ghp->ntgh", s_re, cr) - jnp.einsum("ntgp,ghp->ntgh", s_im, ci)
        return (s_re[:, -1], s_im[:, -1]), y

    (hT_re, hT_im), y = lax.scan(step, (h0_re.astype(f32), h0_im.astype(f32)), ug)
    y = y.swapaxes(0, 1).reshape(n, t, D_MODEL) + uf * d_skip.astype(f32)
    z = jax.nn.gelu(y).astype(u.dtype)
    a, g = jnp.split(z @ w_glu, 2, axis=-1)
    return a * jax.nn.sigmoid(g), hT_re, hT_im


def fox_attention(q, k, v, fq, fk, qpos, kpos):
    n, t, h, hd = q.shape
    blk = ATTN_BLOCK if t % ATTN_BLOCK == 0 else t
    nb = t // blk
    qb = q.reshape(n, nb, blk, h, hd).swapaxes(0, 1)
    fqb = fq.reshape(n, nb, blk, h).swapaxes(0, 1)
    pb = qpos.reshape(nb, blk)
    fk_t = fk.transpose(0, 2, 1).astype(jnp.float32)

    def one_block(args):
        q_blk, fq_blk, p_blk = args
        s = jnp.einsum("nqhd,nshd->nhqs", q_blk, k).astype(jnp.float32) * ATTN_SCALE
        decay = fq_blk.transpose(0, 2, 1).astype(jnp.float32)[..., None] - fk_t[:, :, None, :]
        s = jnp.where(kpos[None, :] <= p_blk[:, None], s + decay, NEG_INF)
        pr = jax.nn.softmax(s, axis=-1).astype(v.dtype)
        return jnp.einsum("nhqs,nshd->nqhd", pr, v)

    out = lax.map(one_block, (qb, fqb, pb))
    return out.swapaxes(0, 1).reshape(n, t, h, hd)


def shared_kv(x, w_k, w_v, w_f, b_f):
    n, t, _ = x.shape
    k = (x @ w_k).reshape(n, t, N_HEADS, HEAD_DIM)
    v = (x @ w_v).reshape(n, t, N_HEADS, HEAD_DIM)
    logf = jax.nn.log_sigmoid((x @ w_f + b_f).astype(jnp.float32))
    return k, v, logf


def trunk(x, c, h0_re, h0_im, key_side, p):
    new_re, new_im = [], []
    kv = None
    k_new = v_new = logf_new = None
    for l in range(DEPTH):
        mod = jax.nn.silu(c) @ p["w_mod"][l] + p["b_mod"][l]
        sh1, sc1, g1, sh2, sc2, g2 = jnp.split(mod[:, None, :], 6, axis=-1)
        h = x * (1.0 + sc1) + sh1
        if l < N_A_LAYERS:
            out, hr, hi = ssm_mixer(h, h0_re[l], h0_im[l], p["ssm_lam_re"][l], p["ssm_lam_im"][l],
                                    p["ssm_log_dt"][l], p["ssm_b_re"][l], p["ssm_b_im"][l],
                                    p["ssm_c_re"][l], p["ssm_c_im"][l], p["ssm_d"][l], p["w_glu"][l])
            new_re.append(hr)
            new_im.append(hi)
        else:
            lb = l - N_A_LAYERS
            k_all, v_all, fk_all, fq, qpos, kpos = kv
            n, t, _ = h.shape
            q = (h @ p["w_q"][lb]).reshape(n, t, N_HEADS, HEAD_DIM)
            o = fox_attention(q, k_all, v_all, fq, fk_all, qpos, kpos)
            out = o.reshape(n, t, D_MODEL) @ p["w_o"][lb]
        x = layer_norm(ALPHA * x + (1.0 + g1) * out, p["ln_g"][l, 0], p["ln_b"][l, 0])
        h = x * (1.0 + sc2) + sh2
        if l % 2 == 0:
            li = l // 2
            out = swiglu(h, p["w_ff_gate"][li], p["w_ff_up"][li], p["w_ff_down"][li])
        else:
            li = l // 2
            out = moe_ffn(h, p["w_router"][li], p["b_router"][li], p["w_exp_gate"][li],
                          p["w_exp_up"][li], p["w_exp_down"][li])
        x = layer_norm(ALPHA * x + (1.0 + g2) * out, p["ln_g"][l, 1], p["ln_b"][l, 1])
        if l == N_A_LAYERS - 1:
            k_new, v_new, logf_new = shared_kv(x, p["w_k"], p["w_v"], p["w_f"], p["b_f"])
            kv = key_side(k_new, v_new, logf_new)
    return x, jnp.stack(new_re), jnp.stack(new_im), k_new, v_new, logf_new


def setup_inputs(seed: int = 0) -> dict:
    key = jax.random.key(seed)
    keys = jax.random.split(key, 40)
    f32 = jnp.float32
    D, G, P, GS, H, HD, F, E = D_MODEL, N_GROUPS, STATE_DIM, GROUP_SIZE, N_HEADS, HEAD_DIM, D_FF, N_EXPERTS
    n_pages = PAST_LEN // PAGE_SIZE
    n_used = DEC_BATCH * n_pages
    n_phys = n_used + (n_used + 3) // 4

    def nrm(i, shape, scale):
        return jax.random.normal(keys[i], shape, f32) * scale

    perm = jax.random.permutation(keys[9], n_phys)
    page_table = perm[:n_used].reshape(DEC_BATCH, n_pages).astype(jnp.int32)
    lam_im0 = math.pi * jnp.arange(P, dtype=f32)
    return {
        "x_prompt": nrm(0, (BATCH, SEQ, D), 1.0),
        "x_sample": nrm(1, (DEC_BATCH, DEC_SEQ, D), 1.0),
        "c_prompt": nrm(2, (BATCH, D), 1.0),
        "c_sample": nrm(3, (DEC_BATCH, D), 1.0),
        "cache_k": nrm(4, (n_phys, PAGE_SIZE, H, HD), 1.0),
        "cache_v": nrm(5, (n_phys, PAGE_SIZE, H, HD), 1.0),
        "cache_logf": jax.nn.log_sigmoid(nrm(6, (n_phys, PAGE_SIZE, H), 1.0) + 3.0),
        "state_ssm_re": nrm(7, (N_A_LAYERS, DEC_BATCH, G, P), 0.1),
        "state_ssm_im": nrm(8, (N_A_LAYERS, DEC_BATCH, G, P), 0.1),
        "page_table": page_table,
        "w_mod": nrm(10, (DEPTH, D, 6 * D), 0.1 * D ** -0.5),
        "b_mod": nrm(11, (DEPTH, 6 * D), 0.01),
        "ln_g": 1.0 + nrm(12, (DEPTH, 2, D), 0.02),
        "ln_b": nrm(13, (DEPTH, 2, D), 0.02),
        "ssm_lam_re": -0.5 + nrm(14, (N_A_LAYERS, G, P), 0.01),
        "ssm_lam_im": lam_im0 + nrm(15, (N_A_LAYERS, G, P), 0.01),
        "ssm_log_dt": jax.random.uniform(keys[16], (N_A_LAYERS, G), f32, math.log(DT_MIN), math.log(DT_MAX)),
        "ssm_b_re": nrm(17, (N_A_LAYERS, G, P, GS), (2 * GS) ** -0.5),
        "ssm_b_im": nrm(18, (N_A_LAYERS, G, P, GS), (2 * GS) ** -0.5),
        "ssm_c_re": nrm(19, (N_A_LAYERS, G, GS, P), (2 * P) ** -0.5),
        "ssm_c_im": nrm(20, (N_A_LAYERS, G, GS, P), (2 * P) ** -0.5),
        "ssm_d": nrm(21, (N_A_LAYERS, D), 1.0),
        "w_glu": nrm(22, (N_A_LAYERS, D, 2 * D), BETA * D ** -0.5),
        "w_k": nrm(23, (D, H * HD), D ** -0.5),
        "w_v": nrm(24, (D, H * HD), D ** -0.5),
        "w_f": nrm(25, (D, H), D ** -0.5),
        "b_f": jax.random.uniform(keys[26], (H,), f32, 1.0, 5.0),
        "w_q": nrm(27, (N_B_LAYERS, D, H * HD), D ** -0.5),
        "w_o": nrm(28, (N_B_LAYERS, H * HD, D), BETA * D ** -0.5),
        "w_ff_gate": nrm(29, (N_DENSE_LAYERS, D, F), D ** -0.5),
        "w_ff_up": nrm(30, (N_DENSE_LAYERS, D, F), D ** -0.5),
        "w_ff_down": nrm(31, (N_DENSE_LAYERS, F, D), BETA * F ** -0.5),
        "w_router": nrm(32, (N_MOE_LAYERS, D, E), D ** -0.5),
        "b_router": nrm(33, (N_MOE_LAYERS, E), 0.01),
        "w_exp_gate": nrm(34, (N_MOE_LAYERS, E, D, F), D ** -0.5),
        "w_exp_up": nrm(35, (N_MOE_LAYERS, E, D, F), D ** -0.5),
        "w_exp_down": nrm(36, (N_MOE_LAYERS, E, F, D), BETA * F ** -0.5),
    }


def reference(x_prompt, x_sample, c_prompt, c_sample, cache_k, cache_v, cache_logf,
              state_ssm_re, state_ssm_im, page_table, w_mod, b_mod, ln_g, ln_b,
              ssm_lam_re, ssm_lam_im, ssm_log_dt, ssm_b_re, ssm_b_im, ssm_c_re, ssm_c_im,
              ssm_d, w_glu, w_k, w_v, w_f, b_f, w_q, w_o, w_ff_gate, w_ff_up, w_ff_down,
              w_router, b_router, w_exp_gate, w_exp_up, w_exp_down):
    p = dict(w_mod=w_mod, b_mod=b_mod, ln_g=ln_g, ln_b=ln_b, ssm_lam_re=ssm_lam_re,
             ssm_lam_im=ssm_lam_im, ssm_log_dt=ssm_log_dt, ssm_b_re=ssm_b_re, ssm_b_im=ssm_b_im,
             ssm_c_re=ssm_c_re, ssm_c_im=ssm_c_im, ssm_d=ssm_d, w_glu=w_glu, w_k=w_k, w_v=w_v,
             w_f=w_f, b_f=b_f, w_q=w_q, w_o=w_o, w_ff_gate=w_ff_gate, w_ff_up=w_ff_up,
             w_ff_down=w_ff_down, w_router=w_router, b_router=b_router, w_exp_gate=w_exp_gate,
             w_exp_up=w_exp_up, w_exp_down=w_exp_down)

    def key_side_prompt(k_new, v_new, logf_new):
        t = k_new.shape[1]
        fk = jnp.cumsum(logf_new, axis=1)
        pos = jnp.arange(t)
        return (k_new, v_new, fk, fk, pos, pos)

    def key_side_sample(k_new, v_new, logf_new):
        nseq, n_pages = page_table.shape
        past = n_pages * PAGE_SIZE
        t = k_new.shape[1]
        k_past = cache_k[page_table].reshape(nseq, past, N_HEADS, HEAD_DIM)
        v_past = cache_v[page_table].reshape(nseq, past, N_HEADS, HEAD_DIM)
        lf_past = cache_logf[page_table].reshape(nseq, past, N_HEADS)
        k_all = jnp.concatenate([k_past, k_new.astype(k_past.dtype)], axis=1)
        v_all = jnp.concatenate([v_past, v_new.astype(v_past.dtype)], axis=1)
        fk = jnp.cumsum(jnp.concatenate([lf_past.astype(jnp.float32), logf_new], axis=1), axis=1)
        kpos = jnp.arange(past + t)
        qpos = past + jnp.arange(t)
        return (k_all, v_all, fk, fk[:, past:], qpos, kpos)

    h0 = jnp.zeros((N_A_LAYERS, x_prompt.shape[0], N_GROUPS, STATE_DIM), jnp.float32)
    y_prompt, ssm_re_p, ssm_im_p, k_p, v_p, logf_p = trunk(x_prompt, c_prompt, h0, h0, key_side_prompt, p)
    y_sample, ssm_re_s, ssm_im_s, k_s, v_s, logf_s = trunk(x_sample, c_sample, state_ssm_re, state_ssm_im,
                                                           key_side_sample, p)
    return (y_prompt, y_sample, ssm_re_p, ssm_im_p, k_p, v_p, logf_p, ssm_re_s, ssm_im_s, k_s, v_s, logf_s)
```

```python
import functools

import jax
import jax.numpy as jnp
from jax import lax
from jax.experimental import pallas as pl
from jax.experimental.pallas import tpu as pltpu

F32 = jnp.float32
BF16 = jnp.bfloat16

GROUP_SIZE = 16
TOP_K = 2
LN_EPS = 1e-5
NEG_INF = -1e30

V7X_MXU_DIM = 256
V7X_LANES = 128
V7X_VMEM_BUDGET = 56 * 1024 * 1024

SSM_CHUNK = 128
ROW_TILE = 512
ATTN_TILE = 512


def _cparams(semantics, vmem_bytes):
    return pltpu.CompilerParams(dimension_semantics=semantics,
                                vmem_limit_bytes=int(min(max(vmem_bytes, 16 * 1024 * 1024), V7X_VMEM_BUDGET)))


def _row_tile(t, pref):
    tile = min(t, pref)
    assert t % tile == 0, (t, tile)
    return tile


def _sigmoid(x):
    return 1.0 / (1.0 + jnp.exp(-x))


def _split_bf16(x, parts):
    out = []
    r = x
    for _ in range(parts):
        p = r.astype(BF16)
        out.append(p)
        r = r - p.astype(F32)
    return out


def _post_norm(x, out, gate, g, b, alpha):
    y = alpha * x + (1.0 + gate) * out
    mu = jnp.mean(y, axis=-1, keepdims=True)
    d = y - mu
    var = jnp.mean(d * d, axis=-1, keepdims=True)
    return d * lax.rsqrt(var + LN_EPS) * g + b


def _mod_kernel(c_ref, w_ref, b_ref, o_ref):
    c = c_ref[...]
    s = (c * _sigmoid(c)).astype(BF16)
    o_ref[0] = jnp.dot(s, w_ref[0].astype(BF16), preferred_element_type=F32) + b_ref[0]


def _mod_call(c_all, w_mod, b_mod):
    depth, d, d6 = w_mod.shape
    r = c_all.shape[0]
    tn = _row_tile(d6, 1536)
    return pl.pallas_call(
        _mod_kernel,
        grid=(depth, d6 // tn),
        in_specs=[pl.BlockSpec((r, d), lambda l, j: (0, 0)),
                  pl.BlockSpec((1, d, tn), lambda l, j: (l, 0, j)),
                  pl.BlockSpec((1, 1, tn), lambda l, j: (l, 0, j))],
        out_specs=pl.BlockSpec((1, r, tn), lambda l, j: (l, 0, j)),
        out_shape=jax.ShapeDtypeStruct((depth, r, d6), F32),
        compiler_params=_cparams(("parallel", "parallel"), 3 * d * tn * 4),
    )(c_all, w_mod, b_mod.reshape(depth, 1, d6))


def _discretise(lr, li, ldt):
    dt = jnp.exp(ldt)
    mag = jnp.exp(lr * dt)
    a_re = mag * jnp.cos(li * dt)
    a_im = mag * jnp.sin(li * dt)
    den = lr * lr + li * li
    nr = a_re - 1.0
    k_re = (nr * lr + a_im * li) / den
    k_im = (a_im * lr - nr * li) / den
    return a_re, a_im, k_re, k_im


def _cmul(ar, ai, br, bi):
    return ar * br - ai * bi, ar * bi + ai * br


def _ssm_prep_kernel(lr_row, li_row, dt_row, lr_col, li_col, dt_col, b_re, b_im,
                     a_re_o, a_im_o, pneg_re_o, pneg_im_o, ppos_re_o, ppos_im_o, bb_re_o, bb_im_o):
    a_re, a_im, _, _ = _discretise(lr_row[...], li_row[...], dt_row[...])
    a_re_o[...] = a_re
    a_im_o[...] = a_im
    _, _, k_re, k_im = _discretise(lr_col[...], li_col[...], dt_col[...])
    br = b_re[...]
    bi = b_im[...]
    bb_re_o[...] = k_re * br - k_im * bi
    bb_im_o[...] = k_re * bi + k_im * br

    chunk = ppos_re_o.shape[0]
    t = lax.broadcasted_iota(jnp.int32, ppos_re_o.shape, 0)
    n2 = a_re * a_re + a_im * a_im
    for (sq_re, sq_im, o_re, o_im) in ((a_re, a_im, ppos_re_o, ppos_im_o),
                                       (a_re / n2, -a_im / n2, pneg_re_o, pneg_im_o)):
        p_re = jnp.ones(ppos_re_o.shape, F32)
        p_im = jnp.zeros(ppos_re_o.shape, F32)
        bit = 1
        while bit < chunk:
            on = (t & bit) != 0
            f_re = jnp.where(on, sq_re, 1.0)
            f_im = jnp.where(on, sq_im, 0.0)
            p_re, p_im = _cmul(p_re, p_im, f_re, f_im)
            sq_re, sq_im = _cmul(sq_re, sq_im, sq_re, sq_im)
            bit *= 2
        o_re[...] = p_re
        o_im[...] = p_im


def _ssm_prep(lam_re, lam_im, log_dt, b_re, b_im, c_re, c_im):
    g, p = lam_re.shape
    gs = b_re.shape[-1]
    gp = g * p
    dt = jnp.broadcast_to(log_dt[:, None], (g, p))
    rows = [a.reshape(1, gp) for a in (lam_re, lam_im, dt)]
    cols = [a.reshape(gp, 1) for a in (lam_re, lam_im, dt)]
    full = lambda shape: pl.BlockSpec(shape, lambda: tuple(0 for _ in shape))
    out_shapes = ([jax.ShapeDtypeStruct((1, gp), F32)] * 2 + [jax.ShapeDtypeStruct((SSM_CHUNK, gp), F32)] * 4
                  + [jax.ShapeDtypeStruct((gp, gs), F32)] * 2)
    a_re, a_im, pneg_re, pneg_im, ppos_re, ppos_im, bb_re, bb_im = pl.pallas_call(
        _ssm_prep_kernel,
        in_specs=[full((1, gp))] * 3 + [full((gp, 1))] * 3 + [full((gp, gs))] * 2,
        out_specs=[full(s.shape) for s in out_shapes],
        out_shape=out_shapes,
        compiler_params=_cparams(None, 48 * SSM_CHUNK * gp * 4),
    )(*rows, *cols, b_re.reshape(gp, gs), b_im.reshape(gp, gs))

    gpb = V7X_MXU_DIM // gs
    nb = g // gpb
    eye = jnp.eye(gpb, dtype=F32)

    def in_proj(bb):
        t = bb.reshape(nb, gpb, p, gs).transpose(0, 1, 3, 2)
        return jnp.einsum("nghp,gk->nghkp", t, eye).reshape(nb, gpb * gs, gpb * p).astype(BF16)

    def out_proj(c):
        t = c.reshape(nb, gpb, gs, p)
        return jnp.einsum("nghp,gk->nkpgh", t, eye).reshape(nb, gpb * p, gpb * gs).astype(BF16)

    return dict(a_re=a_re, a_im=a_im, pneg_re=pneg_re, pneg_im=pneg_im, ppos_re=ppos_re, ppos_im=ppos_im,
                bb_re=in_proj(bb_re), bb_im=in_proj(bb_im), c_re=out_proj(c_re), c_im=out_proj(-c_im))


def _gelu_tanh(y):
    return 0.5 * y * (1.0 + jnp.tanh(0.7978845608028654 * (y + 0.044715 * (y * y * y))))


def _ssm_seq_kernel(x_ref, mod_ref, h0r_ref, h0i_ref, ar_ref, ai_ref, pnr_ref, pni_ref, ppr_ref, ppi_ref,
                    bbr_ref, bbi_ref, cr_ref, ci_ref, d_ref, tri_ref, z_ref, hr_ref, hi_ref):
    step = pl.program_id(1)

    @pl.when(step == 0)
    def _():
        hr_ref[0] = h0r_ref[0]
        hi_ref[0] = h0i_ref[0]

    chunk = x_ref.shape[1]
    u = x_ref[0] * (1.0 + mod_ref[0, 1]) + mod_ref[0, 0]
    ub = u.astype(BF16)
    nb, kin, kst = bbr_ref.shape
    row0 = lax.broadcasted_iota(jnp.int32, (chunk, kst), 0) == 0
    tri = tri_ref[...]
    ys = []
    for n in range(nb):
        cs = slice(n * kst, (n + 1) * kst)
        ubn = ub[:, n * kin:(n + 1) * kin]
        x_re = jnp.dot(ubn, bbr_ref[n], preferred_element_type=F32)
        x_im = jnp.dot(ubn, bbi_ref[n], preferred_element_type=F32)
        c_re, c_im = _cmul(ar_ref[:, cs], ai_ref[:, cs], hr_ref[0, :, cs], hi_ref[0, :, cs])
        x_re = x_re + jnp.where(row0, c_re, 0.0)
        x_im = x_im + jnp.where(row0, c_im, 0.0)
        s_re, s_im = _cmul(x_re, x_im, pnr_ref[:, cs], pni_ref[:, cs])
        cum = []
        for s in (s_re, s_im):
            cum.append(jnp.dot(tri, jnp.concatenate(_split_bf16(s, 2), axis=0), preferred_element_type=F32))
        h_re, h_im = _cmul(cum[0], cum[1], ppr_ref[:, cs], ppi_ref[:, cs])
        hr_ref[0, :, cs] = h_re[chunk - 1:chunk]
        hi_ref[0, :, cs] = h_im[chunk - 1:chunk]
        ys.append(jnp.dot(h_re.astype(BF16), cr_ref[n], preferred_element_type=F32)
                  + jnp.dot(h_im.astype(BF16), ci_ref[n], preferred_element_type=F32))
    y = jnp.concatenate(ys, axis=1) + u * d_ref[...]
    z_ref[0] = _gelu_tanh(y).astype(BF16)


def _ssm_seq_call(x, mod, h0_re, h0_im, sp, d_skip):
    n, t, d = x.shape
    gp = sp["a_re"].shape[1]
    chunk = SSM_CHUNK
    assert t % chunk == 0
    tri = jnp.tril(jnp.ones((chunk, chunk), F32)).astype(BF16)
    tri2 = jnp.concatenate([tri, tri], axis=1)
    const = lambda a: pl.BlockSpec(a.shape, lambda i, j: tuple(0 for _ in a.shape))
    consts = [sp["a_re"], sp["a_im"], sp["pneg_re"], sp["pneg_im"], sp["ppos_re"], sp["ppos_im"],
              sp["bb_re"], sp["bb_im"], sp["c_re"], sp["c_im"], d_skip.reshape(1, d), tri2]
    state = pl.BlockSpec((1, 1, gp), lambda i, j: (i, 0, 0))
    const_bytes = sum(a.size * a.dtype.itemsize for a in consts)
    z, h_re, h_im = pl.pallas_call(
        _ssm_seq_kernel,
        grid=(n, t // chunk),
        in_specs=[pl.BlockSpec((1, chunk, d), lambda i, j: (i, j, 0)),
                  pl.BlockSpec((1, 6, 1, d), lambda i, j: (i, 0, 0, 0)), state, state]
                 + [const(a) for a in consts],
        out_specs=[pl.BlockSpec((1, chunk, d), lambda i, j: (i, j, 0)), state, state],
        out_shape=[jax.ShapeDtypeStruct((n, t, d), BF16),
                   jax.ShapeDtypeStruct((n, 1, gp), F32), jax.ShapeDtypeStruct((n, 1, gp), F32)],
        compiler_params=_cparams(("parallel", "arbitrary"), 2 * const_bytes + 40 * chunk * gp),
    )(x, mod, h0_re.reshape(n, 1, gp), h0_im.reshape(n, 1, gp), *consts)
    return z, h_re.reshape(n, gp), h_im.reshape(n, gp)


def _ssm_step_kernel(x_ref, mod_ref, h0r_ref, h0i_ref, ar_ref, ai_ref, bbr_ref, bbi_ref, cr_ref, ci_ref, d_ref,
                     z_ref, hr_ref, hi_ref):
    u = x_ref[0] * (1.0 + mod_ref[0, 1]) + mod_ref[0, 0]
    ub = u.astype(BF16)
    nb, kin, kst = bbr_ref.shape
    ys = []
    for n in range(nb):
        cs = slice(n * kst, (n + 1) * kst)
        ubn = ub[:, n * kin:(n + 1) * kin]
        c_re, c_im = _cmul(ar_ref[:, cs], ai_ref[:, cs], h0r_ref[:, cs], h0i_ref[:, cs])
        h_re = jnp.dot(ubn, bbr_ref[n], preferred_element_type=F32) + c_re
        h_im = jnp.dot(ubn, bbi_ref[n], preferred_element_type=F32) + c_im
        hr_ref[:, cs] = h_re
        hi_ref[:, cs] = h_im
        ys.append(jnp.dot(h_re.astype(BF16), cr_ref[n], preferred_element_type=F32)
                  + jnp.dot(h_im.astype(BF16), ci_ref[n], preferred_element_type=F32))
    y = jnp.concatenate(ys, axis=1) + u * d_ref[...]
    z_ref[0] = _gelu_tanh(y).astype(BF16)


def _ssm_step_call(x, mod, h0_re, h0_im, sp, d_skip):
    _, rows, d = x.shape
    gp = sp["a_re"].shape[1]
    args = [x, mod, h0_re, h0_im, sp["a_re"], sp["a_im"], sp["bb_re"], sp["bb_im"], sp["c_re"], sp["c_im"],
            d_skip.reshape(1, d)]
    full = lambda a: pl.BlockSpec(a.shape, lambda: tuple(0 for _ in a.shape))
    out_shapes = [jax.ShapeDtypeStruct((1, rows, d), BF16),
                  jax.ShapeDtypeStruct((rows, gp), F32), jax.ShapeDtypeStruct((rows, gp), F32)]
    return pl.pallas_call(
        _ssm_step_kernel,
        in_specs=[full(a) for a in args],
        out_specs=[full(s) for s in out_shapes],
        out_shape=out_shapes,
        compiler_params=_cparams(None, 4 * sum(a.size * a.dtype.itemsize for a in args)),
    )(*args)


def _mixer_post_kernel(a_ref, x_ref, mod_ref, w_ref, lng_ref, lnb_ref, *rest, glu, router, alpha):
    if router:
        wr_ref, br_ref, x1_ref, h2_ref, lg_ref = rest
    else:
        x1_ref, h2_ref = rest
    proj = jnp.dot(a_ref[0], w_ref[...], preferred_element_type=F32)
    if glu:
        d = proj.shape[1] // 2
        proj = proj[:, :d] * _sigmoid(proj[:, d:])
    x1 = _post_norm(x_ref[0], proj, mod_ref[0, 2], lng_ref[...], lnb_ref[...], alpha)
    x1_ref[0] = x1
    h2 = x1 * (1.0 + mod_ref[0, 4]) + mod_ref[0, 3]
    h2_ref[0] = h2.astype(BF16)
    if router:
        h_hi, h_lo = _split_bf16(h2, 2)
        w_hi, w_lo = _split_bf16(wr_ref[...], 2)
        lg = (jnp.dot(h_hi, w_hi, preferred_element_type=F32) + jnp.dot(h_hi, w_lo, preferred_element_type=F32)
              + jnp.dot(h_lo, w_hi, preferred_element_type=F32))
        lg_ref[0] = lg + br_ref[...]


def _mixer_post_call(a, x, mod, w, ln_g, ln_b, alpha, *, glu, w_router=None, b_router=None):
    n, t, d = x.shape
    tm = _row_tile(t, ROW_TILE)
    rmod = mod.shape[2]
    router = w_router is not None
    row = lambda width: pl.BlockSpec((1, tm, width), lambda i, j: (i, j, 0))
    const = lambda arr: pl.BlockSpec(arr.shape, lambda i, j: tuple(0 for _ in arr.shape))
    mod_spec = pl.BlockSpec((1, 6, rmod, d), (lambda i, j: (i, 0, 0, 0)) if rmod == 1 else (lambda i, j: (i, 0, j, 0)))
    args = [a, x, mod, w, ln_g.reshape(1, d), ln_b.reshape(1, d)]
    in_specs = [row(a.shape[2]), row(d), mod_spec, const(w), const(args[4]), const(args[5])]
    out_specs = [row(d), row(d)]
    out_shape = [jax.ShapeDtypeStruct((n, t, d), F32), jax.ShapeDtypeStruct((n, t, d), BF16)]
    if router:
        e = w_router.shape[1]
        args += [w_router, b_router.reshape(1, e)]
        in_specs += [const(w_router), const(args[-1])]
        out_specs.append(row(e))
        out_shape.append(jax.ShapeDtypeStruct((n, t, e), F32))
    vmem = 2 * w.size * 2 + 12 * tm * w.shape[1] * 4 + 8 * tm * d * 4
    return pl.pallas_call(
        functools.partial(_mixer_post_kernel, glu=glu, router=router, alpha=alpha),
        grid=(n, t // tm), in_specs=in_specs, out_specs=out_specs, out_shape=out_shape,
        compiler_params=_cparams(("parallel", "parallel"), vmem),
    )(*args)


def _swiglu_chunks(h, wg_ref, wu_ref, wd_ref, lead, fc):
    f = wg_ref.shape[-1]
    acc = None
    for lo in range(0, f, fc):
        hi = min(lo + fc, f)
        g = jnp.dot(h, wg_ref[lead + (slice(None), slice(lo, hi))], preferred_element_type=F32)
        u = jnp.dot(h, wu_ref[lead + (slice(None), slice(lo, hi))], preferred_element_type=F32)
        a = (g * _sigmoid(g) * u).astype(BF16)
        part = jnp.dot(a, wd_ref[lead + (slice(lo, hi), slice(None))], preferred_element_type=F32)
        acc = part if acc is None else acc + part
    return acc


def _ffn_post_kernel(h_ref, x_ref, mod_ref, wg_ref, wu_ref, wd_ref, lng_ref, lnb_ref, o_ref, *, alpha, fc):
    out = _swiglu_chunks(h_ref[0], wg_ref, wu_ref, wd_ref, (), fc)
    o_ref[0] = _post_norm(x_ref[0], out, mod_ref[0, 5], lng_ref[...], lnb_ref[...], alpha)


def _ffn_post_call(h, x, mod, wg, wu, wd, ln_g, ln_b, alpha):
    n, t, d = x.shape
    f = wg.shape[1]
    tm = _row_tile(t, ROW_TILE)
    rmod = mod.shape[2]
    row = lambda: pl.BlockSpec((1, tm, d), lambda i, j: (i, j, 0))
    const = lambda arr: pl.BlockSpec(arr.shape, lambda i, j: tuple(0 for _ in arr.shape),
                                     pipeline_mode=pl.Buffered(1))
    mod_spec = pl.BlockSpec((1, 6, rmod, d), (lambda i, j: (i, 0, 0, 0)) if rmod == 1 else (lambda i, j: (i, 0, j, 0)))
    fc = 2 * V7X_MXU_DIM
    args = [h, x, mod, wg, wu, wd, ln_g.reshape(1, d), ln_b.reshape(1, d)]
    vmem = 3 * d * f * 2 + 10 * tm * d * 4 + 6 * tm * fc * 4
    return pl.pallas_call(
        functools.partial(_ffn_post_kernel, alpha=alpha, fc=fc),
        grid=(n, t // tm),
        in_specs=[row(), row(), mod_spec] + [const(a) for a in args[3:]],
        out_specs=row(),
        out_shape=jax.ShapeDtypeStruct((n, t, d), F32),
        compiler_params=_cparams(("parallel", "parallel"), vmem),
    )(*args)


def _top2_gates(logits):
    e = logits.shape[-1]
    idx = lax.broadcasted_iota(jnp.int32, logits.shape, 1)
    m1 = jnp.max(logits, axis=-1, keepdims=True)
    i1 = jnp.min(jnp.where(logits == m1, idx, e), axis=-1, keepdims=True)
    rest = jnp.where(idx == i1, -jnp.inf, logits)
    m2 = jnp.max(rest, axis=-1, keepdims=True)
    i2 = jnp.min(jnp.where(rest == m2, idx, e), axis=-1, keepdims=True)
    e2 = jnp.exp(m2 - m1)
    den = 1.0 + e2
    return jnp.where(idx == i1, 1.0 / den, 0.0) + jnp.where(idx == i2, e2 / den, 0.0)


def _moe_post_kernel(h_ref, lg_ref, x_ref, mod_ref, wg_ref, wu_ref, wd_ref, lng_ref, lnb_ref, o_ref,
                     acc_ref, gate_ref, *, alpha, fc):
    ex = pl.program_id(2)
    fs = pl.program_id(3)

    @pl.when((ex == 0) & (fs == 0))
    def _():
        acc_ref[...] = jnp.zeros_like(acc_ref)
        gate_ref[...] = _top2_gates(lg_ref[0])

    idx = lax.broadcasted_iota(jnp.int32, gate_ref.shape, 1)
    gate = jnp.sum(jnp.where(idx == ex, gate_ref[...], 0.0), axis=-1, keepdims=True)
    acc_ref[...] += gate * _swiglu_chunks(h_ref[0], wg_ref, wu_ref, wd_ref, (0,), fc)

    @pl.when((ex == pl.num_programs(2) - 1) & (fs == pl.num_programs(3) - 1))
    def _():
        o_ref[0] = _post_norm(x_ref[0], acc_ref[...], mod_ref[0, 5], lng_ref[...], lnb_ref[...], alpha)


def _moe_post_call(h, logits, x, mod, wg, wu, wd, ln_g, ln_b, alpha):
    n, t, d = x.shape
    e, _, f = wg.shape
    tm = _row_tile(t, ROW_TILE)
    rmod = mod.shape[2]
    fsplit = 2 if f % (2 * V7X_LANES) == 0 else 1
    fb = f // fsplit
    row = lambda width: pl.BlockSpec((1, tm, width), lambda i, j, k, s: (i, j, 0))
    const = lambda arr: pl.BlockSpec(arr.shape, lambda i, j, k, s: tuple(0 for _ in arr.shape))
    mod_spec = pl.BlockSpec((1, 6, rmod, d),
                            (lambda i, j, k, s: (i, 0, 0, 0)) if rmod == 1 else (lambda i, j, k, s: (i, 0, j, 0)))
    args = [h, logits, x, mod, wg, wu, wd, ln_g.reshape(1, d), ln_b.reshape(1, d)]
    vmem = 2 * 3 * d * fb * 2 + 10 * tm * d * 4 + 6 * tm * fb * 4
    return pl.pallas_call(
        functools.partial(_moe_post_kernel, alpha=alpha, fc=fb),
        grid=(n, t // tm, e, fsplit),
        in_specs=[row(d), row(e), row(d), mod_spec,
                  pl.BlockSpec((1, d, fb), lambda i, j, k, s: (k, 0, s)),
                  pl.BlockSpec((1, d, fb), lambda i, j, k, s: (k, 0, s)),
                  pl.BlockSpec((1, fb, d), lambda i, j, k, s: (k, s, 0)),
                  const(args[7]), const(args[8])],
        out_specs=row(d),
        out_shape=jax.ShapeDtypeStruct((n, t, d), F32),
        scratch_shapes=[pltpu.VMEM((tm, d), F32), pltpu.VMEM((tm, e), F32)],
        compiler_params=_cparams(("parallel", "parallel", "arbitrary", "arbitrary"), vmem),
    )(*args)


def _log_sigmoid(z):
    return -(jnp.maximum(-z, 0.0) + jnp.log1p(jnp.exp(-jnp.abs(z))))


def _kv_kernel(x_ref, wk_ref, wv_ref, wf_ref, bf_ref, *rest, cumulative):
    if cumulative:
        tri_ref, k_ref, v_ref, kb_ref, vb_ref, lf_ref, fk_ref, carry_ref = rest
    else:
        k_ref, v_ref, kb_ref, vb_ref, lf_ref = rest
    xb = x_ref[0].astype(BF16)
    k = jnp.dot(xb, wk_ref[...], preferred_element_type=F32)
    v = jnp.dot(xb, wv_ref[...], preferred_element_type=F32)
    k_ref[0] = k
    v_ref[0] = v
    kb_ref[0] = k.astype(BF16)
    vb_ref[0] = v.astype(BF16)
    lf = _log_sigmoid(jnp.dot(xb, wf_ref[...], preferred_element_type=F32) + bf_ref[...])
    lf_ref[0] = lf
    if cumulative:
        @pl.when(pl.program_id(1) == 0)
        def _():
            carry_ref[...] = jnp.zeros_like(carry_ref)

        fk = carry_ref[...] + jnp.dot(tri_ref[...], jnp.concatenate(_split_bf16(lf, 3), axis=0),
                                      preferred_element_type=F32)
        fk_ref[0] = fk
        carry_ref[...] = fk[fk.shape[0] - 1:]


def _kv_call(x, wk, wv, wf, bf, *, cumulative):
    n, t, d = x.shape
    hh = wf.shape[1]
    tm = _row_tile(t, ROW_TILE)
    row = lambda width: pl.BlockSpec((1, tm, width), lambda i, j: (i, j, 0))
    const = lambda arr: pl.BlockSpec(arr.shape, lambda i, j: tuple(0 for _ in arr.shape))
    args = [x, wk, wv, wf, bf.reshape(1, hh)]
    out_specs = [row(d), row(d), row(d), row(d), row(hh)]
    out_shape = [jax.ShapeDtypeStruct((n, t, d), F32)] * 2 + [jax.ShapeDtypeStruct((n, t, d), BF16)] * 2 \
        + [jax.ShapeDtypeStruct((n, t, hh), F32)]
    scratch = []
    if cumulative:
        tri = jnp.tril(jnp.ones((tm, tm), F32)).astype(BF16)
        args.append(jnp.concatenate([tri] * 3, axis=1))
        out_specs.append(row(hh))
        out_shape.append(jax.ShapeDtypeStruct((n, t, hh), F32))
        scratch = [pltpu.VMEM((1, hh), F32)]
    return pl.pallas_call(
        functools.partial(_kv_kernel, cumulative=cumulative),
        grid=(n, t // tm),
        in_specs=[row(d)] + [const(a) for a in args[1:]],
        out_specs=out_specs, out_shape=out_shape, scratch_shapes=scratch,
        compiler_params=_cparams(("parallel", "arbitrary"), 8 * d * d + 24 * tm * d * 4),
    )(*args)


def _q_kernel(x_ref, mod_ref, w_ref, q_ref, *, scale):
    h = (x_ref[0] * (1.0 + mod_ref[0, 1]) + mod_ref[0, 0]).astype(BF16)
    q_ref[0] = (jnp.dot(h, w_ref[...], preferred_element_type=F32) * scale).astype(BF16)


def _q_call(x, mod, wq, scale):
    n, t, d = x.shape
    tm = _row_tile(t, ROW_TILE)
    rmod = mod.shape[2]
    row = lambda: pl.BlockSpec((1, tm, d), lambda i, j: (i, j, 0))
    mod_spec = pl.BlockSpec((1, 6, rmod, d), (lambda i, j: (i, 0, 0, 0)) if rmod == 1 else (lambda i, j: (i, 0, j, 0)))
    return pl.pallas_call(
        functools.partial(_q_kernel, scale=scale),
        grid=(n, t // tm),
        in_specs=[row(), mod_spec, pl.BlockSpec(wq.shape, lambda i, j: (0, 0))],
        out_specs=row(),
        out_shape=jax.ShapeDtypeStruct((n, t, d), BF16),
        compiler_params=_cparams(("parallel", "parallel"), 4 * d * d + 16 * tm * d * 4),
    )(x, mod, wq)


def _attn_kernel(q_ref, k_ref, v_ref, fq_ref, fk_ref, o_ref, m_ref, l_ref, acc_ref, *, heads, hd):
    qi = pl.program_id(1)
    kj = pl.program_id(2)
    tq = q_ref.shape[1]
    tk = k_ref.shape[1]

    @pl.when(kj == 0)
    def _():
        m_ref[...] = jnp.full_like(m_ref, NEG_INF)
        l_ref[...] = jnp.zeros_like(l_ref)
        acc_ref[...] = jnp.zeros_like(acc_ref)

    @pl.when(kj <= qi)
    def _():
        qpos = qi * tq + lax.broadcasted_iota(jnp.int32, (tq, tk), 0)
        kpos = kj * tk + lax.broadcasted_iota(jnp.int32, (tq, tk), 1)
        visible = kpos <= qpos
        for h in range(heads):
            cs = slice(h * hd, (h + 1) * hd)
            s = lax.dot_general(q_ref[0, :, cs], k_ref[0, :, cs], (((1,), (1,)), ((), ())),
                                preferred_element_type=F32)
            s = s + (fq_ref[0, :, h:h + 1] - fk_ref[0, h:h + 1, :])
            s = jnp.where(visible, s, NEG_INF)
            m_prev = m_ref[h]
            m_new = jnp.maximum(m_prev, jnp.max(s, axis=-1, keepdims=True))
            p = jnp.exp(s - m_new)
            alpha = jnp.exp(m_prev - m_new)
            l_ref[h] = alpha * l_ref[h] + jnp.sum(p, axis=-1, keepdims=True)
            acc_ref[:, cs] = alpha * acc_ref[:, cs] + jnp.dot(p.astype(BF16), v_ref[0, :, cs],
                                                              preferred_element_type=F32)
            m_ref[h] = m_new

    @pl.when(kj == qi)
    def _():
        for h in range(heads):
            cs = slice(h * hd, (h + 1) * hd)
            o_ref[0, :, cs] = (acc_ref[:, cs] / l_ref[h]).astype(BF16)


def _attn_call(q, kb, vb, fq, fk_t, heads):
    n, t, d = q.shape
    hd = d // heads
    tq = tk = _row_tile(t, ATTN_TILE)
    kv_spec = pl.BlockSpec((1, tk, d), lambda i, a, b: (i, jnp.minimum(a, b), 0))
    return pl.pallas_call(
        functools.partial(_attn_kernel, heads=heads, hd=hd),
        grid=(n, t // tq, t // tk),
        in_specs=[pl.BlockSpec((1, tq, d), lambda i, a, b: (i, a, 0)), kv_spec, kv_spec,
                  pl.BlockSpec((1, tq, heads), lambda i, a, b: (i, a, 0)),
                  pl.BlockSpec((1, heads, tk), lambda i, a, b: (i, 0, jnp.minimum(a, b)))],
        out_specs=pl.BlockSpec((1, tq, d), lambda i, a, b: (i, a, 0)),
        out_shape=jax.ShapeDtypeStruct((n, t, d), BF16),
        scratch_shapes=[pltpu.VMEM((heads, tq, 1), F32), pltpu.VMEM((heads, tq, 1), F32),
                        pltpu.VMEM((tq, d), F32)],
        compiler_params=_cparams(("parallel", "parallel", "arbitrary"),
                                 2 * heads * tq * V7X_LANES * 4 + 16 * tq * d * 2 + 12 * tq * tk * 4),
    )(q, kb, vb, fq, fk_t)


def _decode_attn_kernel(pt_ref, qcol_ref, k_ref, v_ref, lf_ref, knew_ref, vnew_ref, lfnew_ref, seg_ref, mask_ref,
                        triu_ref, o_ref, qe_ref, m_ref, l_ref, acc_ref, suf_ref):
    del pt_ref
    step = pl.program_id(1)
    seg = seg_ref[...]

    def expand3(a):
        return [jnp.dot(t, seg, preferred_element_type=F32).astype(BF16) for t in _split_bf16(a, 3)]

    @pl.when(step == 0)
    def _():
        qe_ref[...] = qcol_ref[0] * mask_ref[...]
        m_ref[...] = jnp.dot(knew_ref[0].astype(BF16), qe_ref[...], preferred_element_type=F32)
        l_ref[...] = jnp.ones_like(l_ref)
        acc_ref[...] = vnew_ref[0]
        lfn = expand3(lfnew_ref[0])
        suf_ref[...] = lfn[0].astype(F32) + lfn[1].astype(F32) + lfn[2].astype(F32)

    s = jnp.dot(k_ref[0].astype(BF16), qe_ref[...], preferred_element_type=F32)
    lf3 = jnp.concatenate(expand3(lf_ref[0]), axis=0)
    later = jnp.dot(triu_ref[...], lf3, preferred_element_type=F32)
    s = s + later + suf_ref[...]
    m_prev = m_ref[...]
    m_new = jnp.maximum(m_prev, jnp.max(s, axis=0, keepdims=True))
    p = jnp.exp(s - m_new)
    alpha = jnp.exp(m_prev - m_new)
    l_ref[...] = alpha * l_ref[...] + jnp.sum(p, axis=0, keepdims=True)
    pv = p.astype(BF16).astype(F32) * v_ref[0]
    acc_ref[...] = alpha * acc_ref[...] + jnp.sum(pv, axis=0, keepdims=True)
    m_ref[...] = m_new
    ones3 = jnp.ones((8, lf3.shape[0]), BF16)
    suf_ref[...] = suf_ref[...] + jnp.dot(ones3, lf3, preferred_element_type=F32)[0:1]

    @pl.when(step == pl.num_programs(1) - 1)
    def _():
        o_ref[0] = (acc_ref[...] / l_ref[...]).astype(BF16)


def _decode_attn_call(page_table, q, cache_k, cache_v, cache_logf, k_new, v_new, lf_new, heads):
    nseq, npages = page_table.shape
    n_phys, page, _, hd = cache_k.shape
    d = heads * hd
    seg = jnp.repeat(jnp.eye(heads, dtype=F32), hd, axis=1).astype(BF16)
    mask = jnp.kron(jnp.eye(heads, dtype=F32), jnp.ones((hd, hd), F32)).astype(BF16)
    triu = jnp.triu(jnp.ones((page, page), F32), k=1).astype(BF16)
    triu3 = jnp.concatenate([triu] * 3, axis=1)
    paged = lambda width: pl.BlockSpec((1, page, width), lambda b, p, pt: (pt[b, npages - 1 - p], 0, 0))
    per_seq = lambda arr: pl.BlockSpec((1,) + arr.shape[1:], lambda b, p, pt: (b,) + tuple(0 for _ in arr.shape[1:]))
    const = lambda arr: pl.BlockSpec(arr.shape, lambda b, p, pt: tuple(0 for _ in arr.shape))
    args = [q.reshape(nseq, d, 1), cache_k.reshape(n_phys, page, d), cache_v.reshape(n_phys, page, d), cache_logf,
            k_new.reshape(nseq, 1, d), v_new.reshape(nseq, 1, d), lf_new.reshape(nseq, 1, heads), seg, mask, triu3]
    grid_spec = pltpu.PrefetchScalarGridSpec(
        num_scalar_prefetch=1, grid=(nseq, npages),
        in_specs=[per_seq(args[0]), paged(d), paged(d), paged(heads), per_seq(args[4]), per_seq(args[5]),
                  per_seq(args[6]), const(seg), const(mask), const(triu3)],
        out_specs=pl.BlockSpec((1, 1, d), lambda b, p, pt: (b, 0, 0)),
        scratch_shapes=[pltpu.VMEM((d, d), BF16), pltpu.VMEM((1, d), F32), pltpu.VMEM((1, d), F32),
                        pltpu.VMEM((1, d), F32), pltpu.VMEM((1, d), F32)])
    out = pl.pallas_call(
        _decode_attn_kernel, grid_spec=grid_spec,
        out_shape=jax.ShapeDtypeStruct((nseq, 1, d), BF16),
        compiler_params=_cparams(("parallel", "arbitrary"), 24 * page * d * 4 + 6 * d * d * 2),
    )(page_table, *args)
    return out.reshape(1, nseq, d)


def _trunk(x, mods, h0_re, h0_im, wts, ssm, *, sequence, paged=None):
    depth = mods.shape[0]
    n_a = wts["w_glu"].shape[0]
    heads = wts["w_f"].shape[1]
    nb, rows, d = x.shape
    hd = d // heads
    alpha = (2.0 * depth) ** 0.25
    new_re, new_im = [], []
    k_new = v_new = lf_new = None
    kb = vb = fq = fk_t = None
    for l in range(depth):
        mod = mods[l]
        moe = l % 2 == 1
        li = l // 2
        router = dict(w_router=wts["w_router"][li], b_router=wts["b_router"][li]) if moe else {}
        if l < n_a:
            if sequence:
                z, hr, hi = _ssm_seq_call(x, mod, h0_re[l], h0_im[l], ssm[l], wts["ssm_d"][l])
            else:
                z, hr, hi = _ssm_step_call(x, mod, h0_re[l], h0_im[l], ssm[l], wts["ssm_d"][l])
            new_re.append(hr)
            new_im.append(hi)
            res = _mixer_post_call(z, x, mod, wts["w_glu"][l], wts["ln_g"][l, 0], wts["ln_b"][l, 0], alpha,
                                   glu=True, **router)
        else:
            lb = l - n_a
            q = _q_call(x, mod, wts["w_q"][lb], hd ** -0.5)
            if sequence:
                o = _attn_call(q, kb, vb, fq, fk_t, heads)
            else:
                o = _decode_attn_call(paged[0], q[0], paged[1], paged[2], paged[3], k_new[0], v_new[0], lf_new[0],
                                      heads)
            res = _mixer_post_call(o, x, mod, wts["w_o"][lb], wts["ln_g"][l, 0], wts["ln_b"][l, 0], alpha,
                                   glu=False, **router)
        if moe:
            x1, h2, logits = res
            x = _moe_post_call(h2, logits, x1, mod, wts["w_exp_gate"][li], wts["w_exp_up"][li],
                               wts["w_exp_down"][li], wts["ln_g"][l, 1], wts["ln_b"][l, 1], alpha)
        else:
            x1, h2 = res
            x = _ffn_post_call(h2, x1, mod, wts["w_ff_gate"][li], wts["w_ff_up"][li], wts["w_ff_down"][li],
                               wts["ln_g"][l, 1], wts["ln_b"][l, 1], alpha)
        if l == n_a - 1:
            outs = _kv_call(x, wts["w_k"], wts["w_v"], wts["w_f"], wts["b_f"], cumulative=sequence)
            k_new, v_new, kb, vb, lf_new = outs[:5]
            if sequence:
                fq = outs[5]
                fk_t = fq.transpose(0, 2, 1)
    return x, jnp.stack(new_re), jnp.stack(new_im), k_new, v_new, lf_new


def kernel(x_prompt, x_sample, c_prompt, c_sample, cache_k, cache_v, cache_logf, state_ssm_re, state_ssm_im,
           page_table, w_mod, b_mod, ln_g, ln_b, ssm_lam_re, ssm_lam_im, ssm_log_dt, ssm_b_re, ssm_b_im,
           ssm_c_re, ssm_c_im, ssm_d, w_glu, w_k, w_v, w_f, b_f, w_q, w_o, w_ff_gate, w_ff_up, w_ff_down,
           w_router, b_router, w_exp_gate, w_exp_up, w_exp_down):
    batch, seq, d = x_prompt.shape
    dec_batch, dec_seq, _ = x_sample.shape
    assert dec_seq == 1
    depth = w_mod.shape[0]
    n_a, g, p = ssm_lam_re.shape
    heads = w_f.shape[1]
    hd = d // heads
    gp = g * p

    n_c = batch + dec_batch
    c_all = jnp.concatenate([c_prompt, c_sample], axis=0)
    c_all = jnp.pad(c_all, ((0, (-n_c) % 8), (0, 0)))
    mods = _mod_call(c_all, w_mod, b_mod)
    mods_p = mods[:, :batch].reshape(depth, batch, 6, 1, d)
    mods_s = mods[:, batch:n_c].reshape(depth, dec_batch, 6, d).transpose(0, 2, 1, 3)[:, None]

    bf = lambda w: w.astype(BF16)
    wts = dict(ln_g=ln_g, ln_b=ln_b, ssm_d=ssm_d, w_glu=bf(w_glu), w_k=bf(w_k), w_v=bf(w_v), w_f=bf(w_f), b_f=b_f,
               w_q=bf(w_q), w_o=bf(w_o), w_ff_gate=bf(w_ff_gate), w_ff_up=bf(w_ff_up), w_ff_down=bf(w_ff_down),
               w_router=w_router, b_router=b_router, w_exp_gate=bf(w_exp_gate), w_exp_up=bf(w_exp_up),
               w_exp_down=bf(w_exp_down))
    ssm = [_ssm_prep(ssm_lam_re[l], ssm_lam_im[l], ssm_log_dt[l], ssm_b_re[l], ssm_b_im[l], ssm_c_re[l],
                     ssm_c_im[l]) for l in range(n_a)]

    h0 = jnp.zeros((n_a, batch, gp), F32)
    y_p, re_p, im_p, k_p, v_p, lf_p = _trunk(x_prompt, mods_p, h0, h0, wts, ssm, sequence=True)

    x_s = x_sample.reshape(1, dec_batch, d)
    y_s, re_s, im_s, k_s, v_s, lf_s = _trunk(
        x_s, mods_s, state_ssm_re.reshape(n_a, dec_batch, gp), state_ssm_im.reshape(n_a, dec_batch, gp), wts, ssm,
        sequence=False, paged=(page_table, cache_k, cache_v, cache_logf))

    return (y_p, y_s.reshape(dec_batch, 1, d),
            re_p.reshape(n_a, batch, g, p), im_p.reshape(n_a, batch, g, p),
            k_p.reshape(batch, seq, heads, hd), v_p.reshape(batch, seq, heads, hd), lf_p,
            re_s.reshape(n_a, dec_batch, g, p), im_s.reshape(n_a, dec_batch, g, p),
            k_s.reshape(dec_batch, 1, heads, hd), v_s.reshape(dec_batch, 1, heads, hd),
            lf_s.reshape(dec_batch, 1, heads))
```

```python
import functools
import math

import jax
import jax.numpy as jnp
from jax import lax
from jax.experimental import pallas as pl
from jax.experimental.pallas import tpu as pltpu

F32 = jnp.float32
BF16 = jnp.bfloat16

GROUP_SIZE = 16
TOP_K = 2
LN_EPS = 1e-5
NEG_INF = -1e30

V7X_MXU_DIM = 256
V7X_LANES = 128
V7X_VMEM_BUDGET = 56 * 1024 * 1024

SSM_CHUNK = 128
ROW_TILE = 512
ATTN_TILE = 512


def _cparams(semantics, vmem_bytes):
    return pltpu.CompilerParams(dimension_semantics=semantics,
                                vmem_limit_bytes=int(min(max(vmem_bytes, 16 * 1024 * 1024), V7X_VMEM_BUDGET)))


def _row_tile(t, pref):
    tile = min(t, pref)
    assert t % tile == 0, (t, tile)
    return tile


def _sigmoid(x):
    return 1.0 / (1.0 + jnp.exp(-x))


def _split_bf16(x, parts):
    out = []
    r = x
    for _ in range(parts):
        p = r.astype(BF16)
        out.append(p)
        r = r - p.astype(F32)
    return out


def _post_norm(x, out, gate, g, b, alpha):
    y = alpha * x + (1.0 + gate) * out
    mu = jnp.mean(y, axis=-1, keepdims=True)
    d = y - mu
    var = jnp.mean(d * d, axis=-1, keepdims=True)
    return d * lax.rsqrt(var + LN_EPS) * g + b


def _mod_kernel(c_ref, w_ref, b_ref, o_ref):
    c = c_ref[...]
    s = (c * _sigmoid(c)).astype(BF16)
    o_ref[0] = jnp.dot(s, w_ref[0].astype(BF16), preferred_element_type=F32) + b_ref[0]


def _mod_call(c_all, w_mod, b_mod):
    depth, d, d6 = w_mod.shape
    r = c_all.shape[0]
    tn = _row_tile(d6, 1536)
    return pl.pallas_call(
        _mod_kernel,
        grid=(depth, d6 // tn),
        in_specs=[pl.BlockSpec((r, d), lambda l, j: (0, 0)),
                  pl.BlockSpec((1, d, tn), lambda l, j: (l, 0, j)),
                  pl.BlockSpec((1, 1, tn), lambda l, j: (l, 0, j))],
        out_specs=pl.BlockSpec((1, r, tn), lambda l, j: (l, 0, j)),
        out_shape=jax.ShapeDtypeStruct((depth, r, d6), F32),
        name="adaln_mod",
        compiler_params=_cparams(("parallel", "parallel"), 3 * d * tn * 4),
    )(c_all, w_mod, b_mod.reshape(depth, 1, d6))


def _discretise(lr, li, ldt):
    dt = jnp.exp(ldt)
    mag = jnp.exp(lr * dt)
    a_re = mag * jnp.cos(li * dt)
    a_im = mag * jnp.sin(li * dt)
    den = lr * lr + li * li
    nr = a_re - 1.0
    k_re = (nr * lr + a_im * li) / den
    k_im = (a_im * lr - nr * li) / den
    return a_re, a_im, k_re, k_im


def _cmul(ar, ai, br, bi):
    return ar * br - ai * bi, ar * bi + ai * br


def _ssm_prep_kernel(lr_row, li_row, dt_row, lr_col, li_col, dt_col, b_re, b_im,
                     a_re_o, a_im_o, pneg_re_o, pneg_im_o, ppos_re_o, ppos_im_o, bb_re_o, bb_im_o):
    a_re, a_im, _, _ = _discretise(lr_row[...], li_row[...], dt_row[...])
    a_re_o[...] = a_re
    a_im_o[...] = a_im
    _, _, k_re, k_im = _discretise(lr_col[...], li_col[...], dt_col[...])
    br = b_re[...]
    bi = b_im[...]
    bb_re_o[...] = k_re * br - k_im * bi
    bb_im_o[...] = k_re * bi + k_im * br

    chunk = ppos_re_o.shape[0]
    t = lax.broadcasted_iota(jnp.int32, ppos_re_o.shape, 0)
    n2 = a_re * a_re + a_im * a_im
    for (sq_re, sq_im, o_re, o_im) in ((a_re, a_im, ppos_re_o, ppos_im_o),
                                       (a_re / n2, -a_im / n2, pneg_re_o, pneg_im_o)):
        p_re = jnp.ones(ppos_re_o.shape, F32)
        p_im = jnp.zeros(ppos_re_o.shape, F32)
        bit = 1
        while bit < chunk:
            on = (t & bit) != 0
            f_re = jnp.where(on, sq_re, 1.0)
            f_im = jnp.where(on, sq_im, 0.0)
            p_re, p_im = _cmul(p_re, p_im, f_re, f_im)
            sq_re, sq_im = _cmul(sq_re, sq_im, sq_re, sq_im)
            bit *= 2
        o_re[...] = p_re
        o_im[...] = p_im


def _ssm_prep(lam_re, lam_im, log_dt, b_re, b_im, c_re, c_im):
    g, p = lam_re.shape
    gs = b_re.shape[-1]
    gp = g * p
    dt = jnp.broadcast_to(log_dt[:, None], (g, p))
    rows = [a.reshape(1, gp) for a in (lam_re, lam_im, dt)]
    cols = [a.reshape(gp, 1) for a in (lam_re, lam_im, dt)]
    full = lambda shape: pl.BlockSpec(shape, lambda: tuple(0 for _ in shape))
    out_shapes = ([jax.ShapeDtypeStruct((1, gp), F32)] * 2 + [jax.ShapeDtypeStruct((SSM_CHUNK, gp), F32)] * 4
                  + [jax.ShapeDtypeStruct((gp, gs), F32)] * 2)
    a_re, a_im, pneg_re, pneg_im, ppos_re, ppos_im, bb_re, bb_im = pl.pallas_call(
        _ssm_prep_kernel,
        in_specs=[full((1, gp))] * 3 + [full((gp, 1))] * 3 + [full((gp, gs))] * 2,
        out_specs=[full(s.shape) for s in out_shapes],
        out_shape=out_shapes,
        name="ssm_prep",
        compiler_params=_cparams(None, 48 * SSM_CHUNK * gp * 4),
    )(*rows, *cols, b_re.reshape(gp, gs), b_im.reshape(gp, gs))

    gpb = V7X_MXU_DIM // gs
    nb = g // gpb
    eye = jnp.eye(gpb, dtype=F32)

    def in_proj(bb):
        t = bb.reshape(nb, gpb, p, gs).transpose(0, 1, 3, 2)
        return jnp.einsum("nghp,gk->nghkp", t, eye).reshape(nb, gpb * gs, gpb * p).astype(BF16)

    def out_proj(c):
        t = c.reshape(nb, gpb, gs, p)
        return jnp.einsum("nghp,gk->nkpgh", t, eye).reshape(nb, gpb * p, gpb * gs).astype(BF16)

    return dict(a_re=a_re, a_im=a_im, pneg_re=pneg_re, pneg_im=pneg_im, ppos_re=ppos_re, ppos_im=ppos_im,
                bb_re=in_proj(bb_re), bb_im=in_proj(bb_im), c_re=out_proj(c_re), c_im=out_proj(-c_im))


def _gelu_tanh(y):
    return 0.5 * y * (1.0 + jnp.tanh(0.7978845608028654 * (y + 0.044715 * (y * y * y))))


def _ssm_seq_kernel(x_ref, mod_ref, h0r_ref, h0i_ref, ar_ref, ai_ref, pnr_ref, pni_ref, ppr_ref, ppi_ref,
                    bbr_ref, bbi_ref, cr_ref, ci_ref, d_ref, tri_ref, z_ref, hr_ref, hi_ref):
    step = pl.program_id(1)

    @pl.when(step == 0)
    def _():
        hr_ref[0] = h0r_ref[0]
        hi_ref[0] = h0i_ref[0]

    chunk = x_ref.shape[1]
    u = x_ref[0] * (1.0 + mod_ref[0, 1]) + mod_ref[0, 0]
    ub = u.astype(BF16)
    nb, kin, kst = bbr_ref.shape
    row0 = lax.broadcasted_iota(jnp.int32, (chunk, kst), 0) == 0
    tri = tri_ref[...]
    ys = []
    for n in range(nb):
        cs = slice(n * kst, (n + 1) * kst)
        ubn = ub[:, n * kin:(n + 1) * kin]
        x_re = jnp.dot(ubn, bbr_ref[n], preferred_element_type=F32)
        x_im = jnp.dot(ubn, bbi_ref[n], preferred_element_type=F32)
        c_re, c_im = _cmul(ar_ref[:, cs], ai_ref[:, cs], hr_ref[0, :, cs], hi_ref[0, :, cs])
        x_re = x_re + jnp.where(row0, c_re, 0.0)
        x_im = x_im + jnp.where(row0, c_im, 0.0)
        s_re, s_im = _cmul(x_re, x_im, pnr_ref[:, cs], pni_ref[:, cs])
        cum = []
        for s in (s_re, s_im):
            cum.append(jnp.dot(tri, jnp.concatenate(_split_bf16(s, 2), axis=0), preferred_element_type=F32))
        h_re, h_im = _cmul(cum[0], cum[1], ppr_ref[:, cs], ppi_ref[:, cs])
        hr_ref[0, :, cs] = h_re[chunk - 1:chunk]
        hi_ref[0, :, cs] = h_im[chunk - 1:chunk]
        ys.append(jnp.dot(h_re.astype(BF16), cr_ref[n], preferred_element_type=F32)
                  + jnp.dot(h_im.astype(BF16), ci_ref[n], preferred_element_type=F32))
    y = jnp.concatenate(ys, axis=1) + u * d_ref[...]
    z_ref[0] = _gelu_tanh(y).astype(BF16)


def _ssm_seq_call(x, mod, h0_re, h0_im, sp, d_skip):
    n, t, d = x.shape
    gp = sp["a_re"].shape[1]
    chunk = SSM_CHUNK
    assert t % chunk == 0
    tri = jnp.tril(jnp.ones((chunk, chunk), F32)).astype(BF16)
    tri2 = jnp.concatenate([tri, tri], axis=1)
    const = lambda a: pl.BlockSpec(a.shape, lambda i, j: tuple(0 for _ in a.shape))
    consts = [sp["a_re"], sp["a_im"], sp["pneg_re"], sp["pneg_im"], sp["ppos_re"], sp["ppos_im"],
              sp["bb_re"], sp["bb_im"], sp["c_re"], sp["c_im"], d_skip.reshape(1, d), tri2]
    state = pl.BlockSpec((1, 1, gp), lambda i, j: (i, 0, 0))
    const_bytes = sum(a.size * a.dtype.itemsize for a in consts)
    z, h_re, h_im = pl.pallas_call(
        _ssm_seq_kernel,
        grid=(n, t // chunk),
        in_specs=[pl.BlockSpec((1, chunk, d), lambda i, j: (i, j, 0)),
                  pl.BlockSpec((1, 6, 1, d), lambda i, j: (i, 0, 0, 0)), state, state]
                 + [const(a) for a in consts],
        out_specs=[pl.BlockSpec((1, chunk, d), lambda i, j: (i, j, 0)), state, state],
        out_shape=[jax.ShapeDtypeStruct((n, t, d), BF16),
                   jax.ShapeDtypeStruct((n, 1, gp), F32), jax.ShapeDtypeStruct((n, 1, gp), F32)],
        name="ssm_seq",
        compiler_params=_cparams(("parallel", "arbitrary"), 2 * const_bytes + 40 * chunk * gp),
    )(x, mod, h0_re.reshape(n, 1, gp), h0_im.reshape(n, 1, gp), *consts)
    return z, h_re.reshape(n, gp), h_im.reshape(n, gp)


def _ssm_step_kernel(x_ref, mod_ref, h0r_ref, h0i_ref, ar_ref, ai_ref, bbr_ref, bbi_ref, cr_ref, ci_ref, d_ref,
                     z_ref, hr_ref, hi_ref):
    u = x_ref[0] * (1.0 + mod_ref[0, 1]) + mod_ref[0, 0]
    ub = u.astype(BF16)
    nb, kin, kst = bbr_ref.shape
    ys = []
    for n in range(nb):
        cs = slice(n * kst, (n + 1) * kst)
        ubn = ub[:, n * kin:(n + 1) * kin]
        c_re, c_im = _cmul(ar_ref[:, cs], ai_ref[:, cs], h0r_ref[:, cs], h0i_ref[:, cs])
        h_re = jnp.dot(ubn, bbr_ref[n], preferred_element_type=F32) + c_re
        h_im = jnp.dot(ubn, bbi_ref[n], preferred_element_type=F32) + c_im
        hr_ref[:, cs] = h_re
        hi_ref[:, cs] = h_im
        ys.append(jnp.dot(h_re.astype(BF16), cr_ref[n], preferred_element_type=F32)
                  + jnp.dot(h_im.astype(BF16), ci_ref[n], preferred_element_type=F32))
    y = jnp.concatenate(ys, axis=1) + u * d_ref[...]
    z_ref[0] = _gelu_tanh(y).astype(BF16)


def _ssm_step_call(x, mod, h0_re, h0_im, sp, d_skip):
    _, rows, d = x.shape
    gp = sp["a_re"].shape[1]
    args = [x, mod, h0_re, h0_im, sp["a_re"], sp["a_im"], sp["bb_re"], sp["bb_im"], sp["c_re"], sp["c_im"],
            d_skip.reshape(1, d)]
    full = lambda a: pl.BlockSpec(a.shape, lambda: tuple(0 for _ in a.shape))
    out_shapes = [jax.ShapeDtypeStruct((1, rows, d), BF16),
                  jax.ShapeDtypeStruct((rows, gp), F32), jax.ShapeDtypeStruct((rows, gp), F32)]
    return pl.pallas_call(
        _ssm_step_kernel,
        in_specs=[full(a) for a in args],
        out_specs=[full(s) for s in out_shapes],
        out_shape=out_shapes,
        name="ssm_step",
        compiler_params=_cparams(None, 4 * sum(a.size * a.dtype.itemsize for a in args)),
    )(*args)


def _mixer_post_kernel(a_ref, x_ref, mod_ref, w_ref, lng_ref, lnb_ref, *rest, glu, router, alpha):
    if router:
        wr_ref, br_ref, x1_ref, h2_ref, lg_ref = rest
    else:
        x1_ref, h2_ref = rest
    proj = jnp.dot(a_ref[0], w_ref[...], preferred_element_type=F32)
    if glu:
        d = proj.shape[1] // 2
        proj = proj[:, :d] * _sigmoid(proj[:, d:])
    x1 = _post_norm(x_ref[0], proj, mod_ref[0, 2], lng_ref[...], lnb_ref[...], alpha)
    x1_ref[0] = x1
    h2 = x1 * (1.0 + mod_ref[0, 4]) + mod_ref[0, 3]
    h2_ref[0] = h2.astype(BF16)
    if router:
        h_hi, h_lo = _split_bf16(h2, 2)
        w_hi, w_lo = _split_bf16(wr_ref[...], 2)
        lg = (jnp.dot(h_hi, w_hi, preferred_element_type=F32) + jnp.dot(h_hi, w_lo, preferred_element_type=F32)
              + jnp.dot(h_lo, w_hi, preferred_element_type=F32))
        lg_ref[0] = lg + br_ref[...]


def _mixer_post_call(a, x, mod, w, ln_g, ln_b, alpha, *, glu, w_router=None, b_router=None):
    n, t, d = x.shape
    tm = _row_tile(t, ROW_TILE)
    rmod = mod.shape[2]
    router = w_router is not None
    row = lambda width: pl.BlockSpec((1, tm, width), lambda i, j: (i, j, 0))
    const = lambda arr: pl.BlockSpec(arr.shape, lambda i, j: tuple(0 for _ in arr.shape))
    mod_spec = pl.BlockSpec((1, 6, rmod, d), (lambda i, j: (i, 0, 0, 0)) if rmod == 1 else (lambda i, j: (i, 0, j, 0)))
    args = [a, x, mod, w, ln_g.reshape(1, d), ln_b.reshape(1, d)]
    in_specs = [row(a.shape[2]), row(d), mod_spec, const(w), const(args[4]), const(args[5])]
    out_specs = [row(d), row(d)]
    out_shape = [jax.ShapeDtypeStruct((n, t, d), F32), jax.ShapeDtypeStruct((n, t, d), BF16)]
    if router:
        e = w_router.shape[1]
        args += [w_router, b_router.reshape(1, e)]
        in_specs += [const(w_router), const(args[-1])]
        out_specs.append(row(e))
        out_shape.append(jax.ShapeDtypeStruct((n, t, e), F32))
    vmem = 2 * w.size * 2 + 12 * tm * w.shape[1] * 4 + 8 * tm * d * 4
    return pl.pallas_call(
        functools.partial(_mixer_post_kernel, glu=glu, router=router, alpha=alpha),
        grid=(n, t // tm), in_specs=in_specs, out_specs=out_specs, out_shape=out_shape,
        name="mixer_post",
        compiler_params=_cparams(("parallel", "parallel"), vmem),
    )(*args)


def _swiglu_chunks(h, wg_ref, wu_ref, wd_ref, lead, fc):
    f = wg_ref.shape[-1]
    acc = None
    for lo in range(0, f, fc):
        hi = min(lo + fc, f)
        g = jnp.dot(h, wg_ref[lead + (slice(None), slice(lo, hi))], preferred_element_type=F32)
        u = jnp.dot(h, wu_ref[lead + (slice(None), slice(lo, hi))], preferred_element_type=F32)
        a = (g * _sigmoid(g) * u).astype(BF16)
        part = jnp.dot(a, wd_ref[lead + (slice(lo, hi), slice(None))], preferred_element_type=F32)
        acc = part if acc is None else acc + part
    return acc


def _ffn_post_kernel(h_ref, x_ref, mod_ref, wg_ref, wu_ref, wd_ref, lng_ref, lnb_ref, o_ref, *, alpha, fc):
    out = _swiglu_chunks(h_ref[0], wg_ref, wu_ref, wd_ref, (), fc)
    o_ref[0] = _post_norm(x_ref[0], out, mod_ref[0, 5], lng_ref[...], lnb_ref[...], alpha)


def _ffn_post_call(h, x, mod, wg, wu, wd, ln_g, ln_b, alpha):
    n, t, d = x.shape
    f = wg.shape[1]
    tm = _row_tile(t, ROW_TILE)
    rmod = mod.shape[2]
    row = lambda: pl.BlockSpec((1, tm, d), lambda i, j: (i, j, 0))
    const = lambda arr: pl.BlockSpec(arr.shape, lambda i, j: tuple(0 for _ in arr.shape),
                                     pipeline_mode=pl.Buffered(1))
    mod_spec = pl.BlockSpec((1, 6, rmod, d), (lambda i, j: (i, 0, 0, 0)) if rmod == 1 else (lambda i, j: (i, 0, j, 0)))
    fc = 2 * V7X_MXU_DIM
    args = [h, x, mod, wg, wu, wd, ln_g.reshape(1, d), ln_b.reshape(1, d)]
    vmem = 3 * d * f * 2 + 10 * tm * d * 4 + 6 * tm * fc * 4
    return pl.pallas_call(
        functools.partial(_ffn_post_kernel, alpha=alpha, fc=fc),
        grid=(n, t // tm),
        in_specs=[row(), row(), mod_spec] + [const(a) for a in args[3:]],
        out_specs=row(),
        out_shape=jax.ShapeDtypeStruct((n, t, d), F32), name="ffn_post",
        compiler_params=_cparams(("parallel", "parallel"), vmem),
    )(*args)


def _top2_gates(logits):
    e = logits.shape[-1]
    idx = lax.broadcasted_iota(jnp.int32, logits.shape, 1)
    m1 = jnp.max(logits, axis=-1, keepdims=True)
    i1 = jnp.min(jnp.where(logits == m1, idx, e), axis=-1, keepdims=True)
    rest = jnp.where(idx == i1, -jnp.inf, logits)
    m2 = jnp.max(rest, axis=-1, keepdims=True)
    i2 = jnp.min(jnp.where(rest == m2, idx, e), axis=-1, keepdims=True)
    e2 = jnp.exp(m2 - m1)
    den = 1.0 + e2
    return jnp.where(idx == i1, 1.0 / den, 0.0) + jnp.where(idx == i2, e2 / den, 0.0)


def _moe_post_kernel(h_ref, lg_ref, x_ref, mod_ref, wg_ref, wu_ref, wd_ref, lng_ref, lnb_ref, o_ref,
                     acc_ref, gate_ref, *, alpha, fc):
    ex = pl.program_id(2)
    fs = pl.program_id(3)

    @pl.when((ex == 0) & (fs == 0))
    def _():
        acc_ref[...] = jnp.zeros_like(acc_ref)
        gate_ref[...] = _top2_gates(lg_ref[0])

    idx = lax.broadcasted_iota(jnp.int32, gate_ref.shape, 1)
    gate = jnp.sum(jnp.where(idx == ex, gate_ref[...], 0.0), axis=-1, keepdims=True)
    acc_ref[...] += gate * _swiglu_chunks(h_ref[0], wg_ref, wu_ref, wd_ref, (0,), fc)

    @pl.when((ex == pl.num_programs(2) - 1) & (fs == pl.num_programs(3) - 1))
    def _():
        o_ref[0] = _post_norm(x_ref[0], acc_ref[...], mod_ref[0, 5], lng_ref[...], lnb_ref[...], alpha)


def _moe_post_call(h, logits, x, mod, wg, wu, wd, ln_g, ln_b, alpha):
    n, t, d = x.shape
    e, _, f = wg.shape
    tm = _row_tile(t, ROW_TILE)
    rmod = mod.shape[2]
    fsplit = 2 if f % (2 * V7X_LANES) == 0 else 1
    fb = f // fsplit
    row = lambda width: pl.BlockSpec((1, tm, width), lambda i, j, k, s: (i, j, 0))
    const = lambda arr: pl.BlockSpec(arr.shape, lambda i, j, k, s: tuple(0 for _ in arr.shape))
    mod_spec = pl.BlockSpec((1, 6, rmod, d),
                            (lambda i, j, k, s: (i, 0, 0, 0)) if rmod == 1 else (lambda i, j, k, s: (i, 0, j, 0)))
    args = [h, logits, x, mod, wg, wu, wd, ln_g.reshape(1, d), ln_b.reshape(1, d)]
    vmem = 2 * 3 * d * fb * 2 + 10 * tm * d * 4 + 6 * tm * fb * 4
    return pl.pallas_call(
        functools.partial(_moe_post_kernel, alpha=alpha, fc=fb),
        grid=(n, t // tm, e, fsplit),
        in_specs=[row(d), row(e), row(d), mod_spec,
                  pl.BlockSpec((1, d, fb), lambda i, j, k, s: (k, 0, s)),
                  pl.BlockSpec((1, d, fb), lambda i, j, k, s: (k, 0, s)),
                  pl.BlockSpec((1, fb, d), lambda i, j, k, s: (k, s, 0)),
                  const(args[7]), const(args[8])],
        out_specs=row(d),
        out_shape=jax.ShapeDtypeStruct((n, t, d), F32),
        scratch_shapes=[pltpu.VMEM((tm, d), F32), pltpu.VMEM((tm, e), F32)], name="moe_post",
        compiler_params=_cparams(("parallel", "parallel", "arbitrary", "arbitrary"), vmem),
    )(*args)


def _log_sigmoid(z):
    return -(jnp.maximum(-z, 0.0) + jnp.log1p(jnp.exp(-jnp.abs(z))))


ATTN_LANES = V7X_LANES
AUG_TERMS = 3
LOG2E = 1.4426950408889634


def _aug_selectors(heads, hd, offset):
    sel = []
    for i in range(AUG_TERMS):
        col = jnp.arange(heads) * ATTN_LANES + hd + offset + i
        sel.append(jnp.zeros((heads, heads * ATTN_LANES), F32).at[jnp.arange(heads), col].set(1.0).astype(BF16))
    return jnp.stack(sel)


def _aug_ones(heads, hd, offset, count):
    lane = jnp.arange(heads * ATTN_LANES) % ATTN_LANES
    return ((lane >= hd + offset) & (lane < hd + offset + count)).astype(F32).reshape(1, heads * ATTN_LANES)


def _head_major(w, heads):
    d, dh = w.shape
    hd = dh // heads
    return jnp.pad(w.reshape(d, heads, hd), ((0, 0), (0, 0), (0, ATTN_LANES - hd))).reshape(d, heads * ATTN_LANES)


def _place_terms(x, sel_ref, sign):
    out = None
    for i, term in enumerate(_split_bf16(x, AUG_TERMS)):
        part = jnp.dot(term, sel_ref[i], preferred_element_type=F32)
        out = part if out is None else out + part
    return sign * out


def _kv_kernel(x_ref, wk_ref, wv_ref, wf_ref, bf_ref, *rest, cumulative):
    if cumulative:
        (tri_ref, wka_ref, wva_ref, sel_ref, kone_ref, vone_ref,
         k_ref, v_ref, lf_ref, fk_ref, ka_ref, va_ref, carry_ref) = rest
    else:
        k_ref, v_ref, lf_ref = rest
    xb = x_ref[0].astype(BF16)
    k_ref[0] = jnp.dot(xb, wk_ref[...], preferred_element_type=F32)
    v_ref[0] = jnp.dot(xb, wv_ref[...], preferred_element_type=F32)
    lf = _log_sigmoid(jnp.dot(xb, wf_ref[...], preferred_element_type=F32) + bf_ref[...])
    lf_ref[0] = lf
    if cumulative:
        @pl.when(pl.program_id(1) == 0)
        def _():
            carry_ref[...] = jnp.zeros_like(carry_ref)

        fk = carry_ref[...] + jnp.dot(tri_ref[...], jnp.concatenate(_split_bf16(lf, 3), axis=0),
                                      preferred_element_type=F32)
        fk_ref[0] = fk
        carry_ref[...] = fk[fk.shape[0] - 1:]
        ka = (jnp.dot(xb, wka_ref[...], preferred_element_type=F32) + kone_ref[...]
              + _place_terms(fk * LOG2E, sel_ref, -1.0)).astype(BF16)
        va = (jnp.dot(xb, wva_ref[...], preferred_element_type=F32) + vone_ref[...]).astype(BF16)
        for h in range(ka_ref.shape[1]):
            ka_ref[0, h] = ka[:, h * ATTN_LANES:(h + 1) * ATTN_LANES]
            va_ref[0, h] = va[:, h * ATTN_LANES:(h + 1) * ATTN_LANES]


def _kv_call(x, wk, wv, wf, bf, *, cumulative):
    n, t, d = x.shape
    hh = wf.shape[1]
    hd = d // hh
    tm = _row_tile(t, ROW_TILE)
    row = lambda width: pl.BlockSpec((1, tm, width), lambda i, j: (i, j, 0))
    const = lambda arr: pl.BlockSpec(arr.shape, lambda i, j: tuple(0 for _ in arr.shape))
    args = [x, wk, wv, wf, bf.reshape(1, hh)]
    out_specs = [row(d), row(d), row(hh)]
    out_shape = [jax.ShapeDtypeStruct((n, t, d), F32)] * 2 + [jax.ShapeDtypeStruct((n, t, hh), F32)]
    scratch = []
    if cumulative:
        tri = jnp.tril(jnp.ones((tm, tm), F32)).astype(BF16)
        args += [jnp.concatenate([tri] * 3, axis=1), _head_major(wk, hh), _head_major(wv, hh),
                 _aug_selectors(hh, hd, AUG_TERMS), _aug_ones(hh, hd, 0, AUG_TERMS), _aug_ones(hh, hd, 0, 1)]
        head_rows = pl.BlockSpec((1, hh, tm, ATTN_LANES), lambda i, j: (i, 0, j, 0))
        out_specs += [row(hh), head_rows, head_rows]
        out_shape += [jax.ShapeDtypeStruct((n, t, hh), F32)] + [jax.ShapeDtypeStruct((n, hh, t, ATTN_LANES), BF16)] * 2
        scratch = [pltpu.VMEM((1, hh), F32)]
    return pl.pallas_call(
        functools.partial(_kv_kernel, cumulative=cumulative),
        grid=(n, t // tm),
        in_specs=[row(d)] + [const(a) for a in args[1:]],
        out_specs=out_specs, out_shape=out_shape, scratch_shapes=scratch, name="kv_proj",
        compiler_params=_cparams(("parallel", "arbitrary"), 16 * d * d + 40 * tm * d * 4),
    )(*args)


def _q_kernel(x_ref, mod_ref, w_ref, *rest, scale, augmented):
    h = (x_ref[0] * (1.0 + mod_ref[0, 1]) + mod_ref[0, 0]).astype(BF16)
    q = jnp.dot(h, w_ref[...], preferred_element_type=F32) * scale
    if augmented:
        fq_ref, sel_ref, one_ref, q_ref = rest
        qa = (q + one_ref[...] + _place_terms(fq_ref[0] * LOG2E, sel_ref, 1.0)).astype(BF16)
        for hh in range(q_ref.shape[1]):
            q_ref[0, hh] = qa[:, hh * ATTN_LANES:(hh + 1) * ATTN_LANES]
    else:
        q_ref, = rest
        q_ref[0] = q.astype(BF16)


def _q_call(x, mod, wq, heads, fq=None):
    n, t, d = x.shape
    hd = d // heads
    tm = _row_tile(t, ROW_TILE)
    rmod = mod.shape[2]
    row = lambda width: pl.BlockSpec((1, tm, width), lambda i, j: (i, j, 0))
    const = lambda arr: pl.BlockSpec(arr.shape, lambda i, j: tuple(0 for _ in arr.shape))
    mod_spec = pl.BlockSpec((1, 6, rmod, d), (lambda i, j: (i, 0, 0, 0)) if rmod == 1 else (lambda i, j: (i, 0, j, 0)))
    augmented = fq is not None
    if augmented:
        args = [x, mod, _head_major(wq, heads), fq, _aug_selectors(heads, hd, 0),
                _aug_ones(heads, hd, AUG_TERMS, AUG_TERMS)]
        in_specs = [row(d), mod_spec, const(args[2]), row(heads), const(args[4]), const(args[5])]
        out_specs = pl.BlockSpec((1, heads, tm, ATTN_LANES), lambda i, j: (i, 0, j, 0))
        out_shape = jax.ShapeDtypeStruct((n, heads, t, ATTN_LANES), BF16)
        scale = hd ** -0.5 * LOG2E
    else:
        args = [x, mod, wq]
        in_specs = [row(d), mod_spec, const(wq)]
        out_specs = row(d)
        out_shape = jax.ShapeDtypeStruct((n, t, d), BF16)
        scale = hd ** -0.5
    return pl.pallas_call(
        functools.partial(_q_kernel, scale=scale, augmented=augmented),
        grid=(n, t // tm), in_specs=in_specs, out_specs=out_specs, out_shape=out_shape, name="q_proj",
        compiler_params=_cparams(("parallel", "parallel"), 8 * d * d + 24 * tm * d * 4),
    )(*args)


ATTN_ROW_BLOCK = 128
ATTN_HEAD_GROUP = 8


def _attn_kernel(q_ref, k_ref, v_ref, o_ref, m_ref, acc_ref, *, hd):
    qi = pl.program_id(1)
    kj = pl.program_id(2)
    heads, tq = q_ref.shape[1], q_ref.shape[2]
    tk = k_ref.shape[2]
    rb = min(ATTN_ROW_BLOCK, tq)
    group = math.gcd(heads, ATTN_HEAD_GROUP)
    nt = (((1,), (1,)), ((), ()))

    @pl.when(kj == 0)
    def _():
        m_ref[...] = jnp.full_like(m_ref, NEG_INF)
        acc_ref[...] = jnp.zeros_like(acc_ref)

    def sweep(diagonal):
        blocks = [slice(r * rb, (r + 1) * rb) for r in range(tq // rb)]

        def group_body(g, carry):
            hs = [g * group + i for i in range(group)]
            scores = [[lax.dot_general(q_ref[0, h, rows, :], k_ref[0, h], nt, preferred_element_type=F32)
                       for rows in blocks] for h in hs]
            for h, s_h in zip(hs, scores):
                vh = v_ref[0, h]
                m_all = m_ref[h]
                acc_all = acc_ref[h]
                m_out, acc_out = [], []
                for r, (rows, s) in enumerate(zip(blocks, s_h)):
                    if diagonal:
                        qpos = r * rb + lax.broadcasted_iota(jnp.int32, (rb, tk), 0)
                        kpos = lax.broadcasted_iota(jnp.int32, (rb, tk), 1)
                        s = jnp.where(kpos <= qpos, s, NEG_INF)
                    m_prev = m_all[rows]
                    m_new = jnp.maximum(m_prev, jnp.max(s, axis=-1, keepdims=True))
                    p = jnp.exp2(s - m_new)
                    acc_out.append(jnp.exp2(m_prev - m_new) * acc_all[rows]
                                   + jnp.dot(p.astype(BF16), vh, preferred_element_type=F32))
                    m_out.append(m_new)
                m_ref[h] = jnp.concatenate(m_out, axis=0)
                acc_ref[h] = jnp.concatenate(acc_out, axis=0)
            return carry
        lax.fori_loop(0, heads // group, group_body, 0)

    @pl.when(kj < qi)
    def _():
        sweep(False)

    @pl.when(kj == qi)
    def _():
        sweep(True)
        for h in range(heads):
            a = acc_ref[h]
            o_ref[0, :, h * hd:(h + 1) * hd] = (a[:, :hd] / a[:, hd:hd + 1]).astype(BF16)


def _attn_call(qa, ka, va, hd):
    n, heads, t, lanes = qa.shape
    tq = tk = _row_tile(t, ATTN_TILE)
    kv_spec = pl.BlockSpec((1, heads, tk, lanes), lambda i, a, b: (i, 0, jnp.minimum(a, b), 0))
    return pl.pallas_call(
        functools.partial(_attn_kernel, hd=hd),
        grid=(n, t // tq, t // tk),
        in_specs=[pl.BlockSpec((1, heads, tq, lanes), lambda i, a, b: (i, 0, a, 0)), kv_spec, kv_spec],
        out_specs=pl.BlockSpec((1, tq, heads * hd), lambda i, a, b: (i, a, 0)),
        out_shape=jax.ShapeDtypeStruct((n, t, heads * hd), BF16),
        scratch_shapes=[pltpu.VMEM((heads, tq, 1), F32), pltpu.VMEM((heads, tq, lanes), F32)],
        name="causal_attn",
        compiler_params=_cparams(("parallel", "parallel", "arbitrary"),
                                 2 * heads * tq * lanes * 4 + 8 * heads * tq * lanes * 2 + 16 * tq * tk * 4),
    )(qa, ka, va)


def _column_to_row(col, eye):
    return jnp.sum(jnp.where(eye > 0, col, 0.0), axis=0, keepdims=True)


DECODE_PAGES_PER_STEP = 8


def _decode_attn_kernel(pt_ref, qb_ref, *rest, g):
    del pt_ref
    kt_refs, vt_refs, lft_refs = rest[:g], rest[g:2 * g], rest[2 * g:3 * g]
    (knew_ref, vnew_ref, lfnew_ref, later_ref, diag_ref, eye_ref, o_ref, m_ref, l_ref, acc_ref, suf_ref) = rest[3 * g:]
    step = pl.program_id(1)
    qb = qb_ref[0]
    eye = eye_ref[...]

    @pl.when(step == 0)
    def _():
        m_ref[...] = jnp.dot(qb, knew_ref[0].astype(BF16), preferred_element_type=F32)
        l_ref[...] = jnp.ones_like(l_ref)
        acc_ref[...] = jnp.broadcast_to(vnew_ref[0], acc_ref.shape)
        suf_ref[...] = lfnew_ref[0]

    scores = []
    after = suf_ref[...]
    for kt_ref, lft_ref in zip(kt_refs, lft_refs):
        lf = lft_ref[0]
        lf3 = jnp.concatenate(_split_bf16(lf, 3), axis=1)
        scores.append(jnp.dot(qb, kt_ref[0].astype(BF16), preferred_element_type=F32)
                      + jnp.dot(lf3, later_ref[...], preferred_element_type=F32) + after)
        after = after + jnp.sum(lf, axis=1, keepdims=True)
    suf_ref[...] = after
    m_prev = m_ref[...]
    m_new = m_prev
    for s in scores:
        m_new = jnp.maximum(m_new, jnp.max(s, axis=1, keepdims=True))
    alpha = jnp.exp(m_prev - m_new)
    l_new = alpha * l_ref[...]
    acc = _column_to_row(alpha, eye) * acc_ref[...]
    for s, vt_ref in zip(scores, vt_refs):
        p = jnp.exp(s - m_new)
        l_new = l_new + jnp.sum(p, axis=1, keepdims=True)
        acc = acc + lax.dot_general(vt_ref[0].astype(BF16), p.astype(BF16), (((1,), (1,)), ((), ())),
                                    preferred_element_type=F32)
    l_ref[...] = l_new
    acc_ref[...] = acc
    m_ref[...] = m_new

    @pl.when(step == pl.num_programs(1) - 1)
    def _():
        on_diag = diag_ref[...] > 0
        num = jnp.sum(jnp.where(on_diag, acc_ref[...], 0.0), axis=1, keepdims=True)
        den = jnp.sum(jnp.where(on_diag, _column_to_row(l_ref[...], eye), 0.0), axis=1, keepdims=True)
        o_ref[0] = (num / den).astype(BF16)


def _decode_attn_call(page_table, q, cache_k, cache_v, cache_logf, k_new, v_new, lf_new, heads):
    nseq, npages = page_table.shape
    n_phys, page, _, hd = cache_k.shape
    d = heads * hd
    seg = jnp.repeat(jnp.eye(heads, dtype=F32), hd, axis=1)
    qb = q[:, None, :] * seg[None].astype(BF16)
    later = jnp.tril(jnp.ones((page, page), F32), k=-1).astype(BF16)
    later3 = jnp.concatenate([later] * 3, axis=0)
    kt = cache_k.transpose(0, 2, 3, 1).reshape(n_phys, d, page)
    vt = cache_v.transpose(0, 2, 3, 1).reshape(n_phys, d, page)
    lft = cache_logf.transpose(0, 2, 1)
    g = math.gcd(npages, DECODE_PAGES_PER_STEP)

    def paged(rows):
        return [pl.BlockSpec((1, rows, page), functools.partial(
            lambda b, p, pt, i: (pt[b, npages - 1 - (p * g + i)], 0, 0), i=i)) for i in range(g)]

    per_seq = lambda arr: pl.BlockSpec((1,) + arr.shape[1:], lambda b, p, pt: (b,) + tuple(0 for _ in arr.shape[1:]))
    const = lambda arr: pl.BlockSpec(arr.shape, lambda b, p, pt: tuple(0 for _ in arr.shape))
    tail = [k_new.reshape(nseq, d, 1), v_new.reshape(nseq, d, 1), lf_new.reshape(nseq, heads, 1),
            later3, seg.T, jnp.eye(heads, dtype=F32)]
    args = [qb] + [kt] * g + [vt] * g + [lft] * g + tail
    grid_spec = pltpu.PrefetchScalarGridSpec(
        num_scalar_prefetch=1, grid=(nseq, npages // g),
        in_specs=[per_seq(qb)] + paged(d) + paged(d) + paged(heads)
                 + [per_seq(a) for a in tail[:3]] + [const(a) for a in tail[3:]],
        out_specs=pl.BlockSpec((1, d, 1), lambda b, p, pt: (b, 0, 0)),
        scratch_shapes=[pltpu.VMEM((heads, 1), F32), pltpu.VMEM((heads, 1), F32), pltpu.VMEM((d, heads), F32),
                        pltpu.VMEM((heads, 1), F32)])
    out = pl.pallas_call(
        functools.partial(_decode_attn_kernel, g=g), grid_spec=grid_spec,
        out_shape=jax.ShapeDtypeStruct((nseq, d, 1), BF16), name="paged_decode_attn",
        compiler_params=_cparams(("parallel", "arbitrary"), (8 + 7 * g) * page * d * 4),
    )(page_table, *args)
    return out.reshape(1, nseq, d)


def _trunk(x, mods, h0_re, h0_im, wts, ssm, *, sequence, paged=None):
    depth = mods.shape[0]
    n_a = wts["w_glu"].shape[0]
    heads = wts["w_f"].shape[1]
    nb, rows, d = x.shape
    hd = d // heads
    alpha = (2.0 * depth) ** 0.25
    new_re, new_im = [], []
    k_new = v_new = lf_new = None
    fk = k_aug = v_aug = None
    for l in range(depth):
        mod = mods[l]
        moe = l % 2 == 1
        li = l // 2
        router = dict(w_router=wts["w_router"][li], b_router=wts["b_router"][li]) if moe else {}
        if l < n_a:
            if sequence:
                z, hr, hi = _ssm_seq_call(x, mod, h0_re[l], h0_im[l], ssm[l], wts["ssm_d"][l])
            else:
                z, hr, hi = _ssm_step_call(x, mod, h0_re[l], h0_im[l], ssm[l], wts["ssm_d"][l])
            new_re.append(hr)
            new_im.append(hi)
            res = _mixer_post_call(z, x, mod, wts["w_glu"][l], wts["ln_g"][l, 0], wts["ln_b"][l, 0], alpha,
                                   glu=True, **router)
        else:
            lb = l - n_a
            if sequence:
                o = _attn_call(_q_call(x, mod, wts["w_q"][lb], heads, fq=fk), k_aug, v_aug, hd)
            else:
                q = _q_call(x, mod, wts["w_q"][lb], heads)
                o = _decode_attn_call(paged[0], q[0], paged[1], paged[2], paged[3], k_new[0], v_new[0], lf_new[0],
                                      heads)
            res = _mixer_post_call(o, x, mod, wts["w_o"][lb], wts["ln_g"][l, 0], wts["ln_b"][l, 0], alpha,
                                   glu=False, **router)
        if moe:
            x1, h2, logits = res
            x = _moe_post_call(h2, logits, x1, mod, wts["w_exp_gate"][li], wts["w_exp_up"][li],
                               wts["w_exp_down"][li], wts["ln_g"][l, 1], wts["ln_b"][l, 1], alpha)
        else:
            x1, h2 = res
            x = _ffn_post_call(h2, x1, mod, wts["w_ff_gate"][li], wts["w_ff_up"][li], wts["w_ff_down"][li],
                               wts["ln_g"][l, 1], wts["ln_b"][l, 1], alpha)
        if l == n_a - 1:
            outs = _kv_call(x, wts["w_k"], wts["w_v"], wts["w_f"], wts["b_f"], cumulative=sequence)
            k_new, v_new, lf_new = outs[:3]
            if sequence:
                fk, k_aug, v_aug = outs[3:]
    return x, jnp.stack(new_re), jnp.stack(new_im), k_new, v_new, lf_new


def kernel(x_prompt, x_sample, c_prompt, c_sample, cache_k, cache_v, cache_logf, state_ssm_re, state_ssm_im,
           page_table, w_mod, b_mod, ln_g, ln_b, ssm_lam_re, ssm_lam_im, ssm_log_dt, ssm_b_re, ssm_b_im,
           ssm_c_re, ssm_c_im, ssm_d, w_glu, w_k, w_v, w_f, b_f, w_q, w_o, w_ff_gate, w_ff_up, w_ff_down,
           w_router, b_router, w_exp_gate, w_exp_up, w_exp_down):
    batch, seq, d = x_prompt.shape
    dec_batch, dec_seq, _ = x_sample.shape
    assert dec_seq == 1
    depth = w_mod.shape[0]
    n_a, g, p = ssm_lam_re.shape
    heads = w_f.shape[1]
    hd = d // heads
    gp = g * p

    n_c = batch + dec_batch
    c_all = jnp.concatenate([c_prompt, c_sample], axis=0)
    c_all = jnp.pad(c_all, ((0, (-n_c) % 8), (0, 0)))
    mods = _mod_call(c_all, w_mod, b_mod)
    mods_p = mods[:, :batch].reshape(depth, batch, 6, 1, d)
    mods_s = mods[:, batch:n_c].reshape(depth, dec_batch, 6, d).transpose(0, 2, 1, 3)[:, None]

    bf = lambda w: w.astype(BF16)
    wts = dict(ln_g=ln_g, ln_b=ln_b, ssm_d=ssm_d, w_glu=bf(w_glu), w_k=bf(w_k), w_v=bf(w_v), w_f=bf(w_f), b_f=b_f,
               w_q=bf(w_q), w_o=bf(w_o), w_ff_gate=bf(w_ff_gate), w_ff_up=bf(w_ff_up), w_ff_down=bf(w_ff_down),
               w_router=w_router, b_router=b_router, w_exp_gate=bf(w_exp_gate), w_exp_up=bf(w_exp_up),
               w_exp_down=bf(w_exp_down))
    ssm = [_ssm_prep(ssm_lam_re[l], ssm_lam_im[l], ssm_log_dt[l], ssm_b_re[l], ssm_b_im[l], ssm_c_re[l],
                     ssm_c_im[l]) for l in range(n_a)]

    h0 = jnp.zeros((n_a, batch, gp), F32)
    y_p, re_p, im_p, k_p, v_p, lf_p = _trunk(x_prompt, mods_p, h0, h0, wts, ssm, sequence=True)

    x_s = x_sample.reshape(1, dec_batch, d)
    y_s, re_s, im_s, k_s, v_s, lf_s = _trunk(
        x_s, mods_s, state_ssm_re.reshape(n_a, dec_batch, gp), state_ssm_im.reshape(n_a, dec_batch, gp), wts, ssm,
        sequence=False, paged=(page_table, cache_k, cache_v, cache_logf))

    return (y_p, y_s.reshape(dec_batch, 1, d),
            re_p.reshape(n_a, batch, g, p), im_p.reshape(n_a, batch, g, p),
            k_p.reshape(batch, seq, heads, hd), v_p.reshape(batch, seq, heads, hd), lf_p,
            re_s.reshape(n_a, dec_batch, g, p), im_s.reshape(n_a, dec_batch, g, p),
            k_s.reshape(dec_batch, 1, heads, hd), v_s.reshape(dec_batch, 1, heads, hd),
            lf_s.reshape(dec_batch, 1, heads))
```

```python
import functools
import math

import jax
import jax.numpy as jnp
from jax import lax
from jax.experimental import pallas as pl
from jax.experimental.pallas import tpu as pltpu

F32 = jnp.float32
BF16 = jnp.bfloat16

GROUP_SIZE = 16
TOP_K = 2
LN_EPS = 1e-5
NEG_INF = -1e30

V7X_MXU_DIM = 256
V7X_LANES = 128
V7X_VMEM_BUDGET = 56 * 1024 * 1024

SSM_CHUNK = 128
ROW_TILE = 512
ATTN_TILE = 512


def _cparams(semantics, vmem_bytes):
    return pltpu.CompilerParams(dimension_semantics=semantics,
                                vmem_limit_bytes=int(min(max(vmem_bytes, 16 * 1024 * 1024), V7X_VMEM_BUDGET)))


def _row_tile(t, pref):
    tile = min(t, pref)
    assert t % tile == 0, (t, tile)
    return tile


def _sigmoid(x):
    return 1.0 / (1.0 + jnp.exp(-x))


def _split_bf16(x, parts):
    out = []
    r = x
    for _ in range(parts):
        p = r.astype(BF16)
        out.append(p)
        r = r - p.astype(F32)
    return out


def _post_norm(x, out, gate, g, b, alpha):
    y = alpha * x + (1.0 + gate) * out
    mu = jnp.mean(y, axis=-1, keepdims=True)
    d = y - mu
    var = jnp.mean(d * d, axis=-1, keepdims=True)
    return d * lax.rsqrt(var + LN_EPS) * g + b


def _mod_kernel(c_ref, w_ref, b_ref, o_ref):
    c = c_ref[...]
    s = (c * _sigmoid(c)).astype(BF16)
    o_ref[0] = jnp.dot(s, w_ref[0].astype(BF16), preferred_element_type=F32) + b_ref[0]


def _mod_call(c_all, w_mod, b_mod):
    depth, d, d6 = w_mod.shape
    r = c_all.shape[0]
    tn = _row_tile(d6, 1536)
    return pl.pallas_call(
        _mod_kernel,
        grid=(depth, d6 // tn),
        in_specs=[pl.BlockSpec((r, d), lambda l, j: (0, 0)),
                  pl.BlockSpec((1, d, tn), lambda l, j: (l, 0, j)),
                  pl.BlockSpec((1, 1, tn), lambda l, j: (l, 0, j))],
        out_specs=pl.BlockSpec((1, r, tn), lambda l, j: (l, 0, j)),
        out_shape=jax.ShapeDtypeStruct((depth, r, d6), F32),
        name="adaln_mod",
        compiler_params=_cparams(("parallel", "parallel"), 3 * d * tn * 4),
    )(c_all, w_mod, b_mod.reshape(depth, 1, d6))


def _discretise(lr, li, ldt):
    dt = jnp.exp(ldt)
    mag = jnp.exp(lr * dt)
    a_re = mag * jnp.cos(li * dt)
    a_im = mag * jnp.sin(li * dt)
    den = lr * lr + li * li
    nr = a_re - 1.0
    k_re = (nr * lr + a_im * li) / den
    k_im = (a_im * lr - nr * li) / den
    return a_re, a_im, k_re, k_im


def _cmul(ar, ai, br, bi):
    return ar * br - ai * bi, ar * bi + ai * br


def _ssm_prep_kernel(lr_row, li_row, dt_row, lr_col, li_col, dt_col, b_re, b_im,
                     a_re_o, a_im_o, pneg_re_o, pneg_im_o, ppos_re_o, ppos_im_o, bb_re_o, bb_im_o):
    a_re, a_im, _, _ = _discretise(lr_row[...], li_row[...], dt_row[...])
    a_re_o[...] = a_re
    a_im_o[...] = a_im
    _, _, k_re, k_im = _discretise(lr_col[...], li_col[...], dt_col[...])
    br = b_re[...]
    bi = b_im[...]
    bb_re_o[...] = k_re * br - k_im * bi
    bb_im_o[...] = k_re * bi + k_im * br

    chunk = ppos_re_o.shape[0]
    t = lax.broadcasted_iota(jnp.int32, ppos_re_o.shape, 0)
    n2 = a_re * a_re + a_im * a_im
    for (sq_re, sq_im, o_re, o_im) in ((a_re, a_im, ppos_re_o, ppos_im_o),
                                       (a_re / n2, -a_im / n2, pneg_re_o, pneg_im_o)):
        p_re = jnp.ones(ppos_re_o.shape, F32)
        p_im = jnp.zeros(ppos_re_o.shape, F32)
        bit = 1
        while bit < chunk:
            on = (t & bit) != 0
            f_re = jnp.where(on, sq_re, 1.0)
            f_im = jnp.where(on, sq_im, 0.0)
            p_re, p_im = _cmul(p_re, p_im, f_re, f_im)
            sq_re, sq_im = _cmul(sq_re, sq_im, sq_re, sq_im)
            bit *= 2
        o_re[...] = p_re
        o_im[...] = p_im


def _ssm_prep(lam_re, lam_im, log_dt, b_re, b_im, c_re, c_im):
    g, p = lam_re.shape
    gs = b_re.shape[-1]
    gp = g * p
    dt = jnp.broadcast_to(log_dt[:, None], (g, p))
    rows = [a.reshape(1, gp) for a in (lam_re, lam_im, dt)]
    cols = [a.reshape(gp, 1) for a in (lam_re, lam_im, dt)]
    full = lambda shape: pl.BlockSpec(shape, lambda: tuple(0 for _ in shape))
    out_shapes = ([jax.ShapeDtypeStruct((1, gp), F32)] * 2 + [jax.ShapeDtypeStruct((SSM_CHUNK, gp), F32)] * 4
                  + [jax.ShapeDtypeStruct((gp, gs), F32)] * 2)
    a_re, a_im, pneg_re, pneg_im, ppos_re, ppos_im, bb_re, bb_im = pl.pallas_call(
        _ssm_prep_kernel,
        in_specs=[full((1, gp))] * 3 + [full((gp, 1))] * 3 + [full((gp, gs))] * 2,
        out_specs=[full(s.shape) for s in out_shapes],
        out_shape=out_shapes,
        name="ssm_prep",
        compiler_params=_cparams(None, 48 * SSM_CHUNK * gp * 4),
    )(*rows, *cols, b_re.reshape(gp, gs), b_im.reshape(gp, gs))

    gpb = V7X_MXU_DIM // gs
    nb = g // gpb
    eye = jnp.eye(gpb, dtype=F32)

    def in_proj(bb):
        t = bb.reshape(nb, gpb, p, gs).transpose(0, 1, 3, 2)
        return jnp.einsum("nghp,gk->nghkp", t, eye).reshape(nb, gpb * gs, gpb * p).astype(BF16)

    def out_proj(c):
        t = c.reshape(nb, gpb, gs, p)
        return jnp.einsum("nghp,gk->nkpgh", t, eye).reshape(nb, gpb * p, gpb * gs).astype(BF16)

    return dict(a_re=a_re, a_im=a_im, pneg_re=pneg_re, pneg_im=pneg_im, ppos_re=ppos_re, ppos_im=ppos_im,
                bb_re=in_proj(bb_re), bb_im=in_proj(bb_im), c_re=out_proj(c_re), c_im=out_proj(-c_im))


def _gelu_tanh(y):
    return 0.5 * y * (1.0 + jnp.tanh(0.7978845608028654 * (y + 0.044715 * (y * y * y))))


def _ssm_seq_kernel(x_ref, mod_ref, h0r_ref, h0i_ref, ar_ref, ai_ref, pnr_ref, pni_ref, ppr_ref, ppi_ref,
                    bbr_ref, bbi_ref, cr_ref, ci_ref, d_ref, tri_ref, z_ref, hr_ref, hi_ref):
    step = pl.program_id(1)

    @pl.when(step == 0)
    def _():
        hr_ref[0] = h0r_ref[0]
        hi_ref[0] = h0i_ref[0]

    chunk = x_ref.shape[1]
    u = x_ref[0] * (1.0 + mod_ref[0, 1]) + mod_ref[0, 0]
    ub = u.astype(BF16)
    nb, kin, kst = bbr_ref.shape
    row0 = lax.broadcasted_iota(jnp.int32, (chunk, kst), 0) == 0
    tri = tri_ref[...]
    ys = []
    for n in range(nb):
        cs = slice(n * kst, (n + 1) * kst)
        ubn = ub[:, n * kin:(n + 1) * kin]
        x_re = jnp.dot(ubn, bbr_ref[n], preferred_element_type=F32)
        x_im = jnp.dot(ubn, bbi_ref[n], preferred_element_type=F32)
        c_re, c_im = _cmul(ar_ref[:, cs], ai_ref[:, cs], hr_ref[0, :, cs], hi_ref[0, :, cs])
        x_re = x_re + jnp.where(row0, c_re, 0.0)
        x_im = x_im + jnp.where(row0, c_im, 0.0)
        s_re, s_im = _cmul(x_re, x_im, pnr_ref[:, cs], pni_ref[:, cs])
        cum = []
        for s in (s_re, s_im):
            cum.append(jnp.dot(tri, jnp.concatenate(_split_bf16(s, 2), axis=0), preferred_element_type=F32))
        h_re, h_im = _cmul(cum[0], cum[1], ppr_ref[:, cs], ppi_ref[:, cs])
        hr_ref[0, :, cs] = h_re[chunk - 1:chunk]
        hi_ref[0, :, cs] = h_im[chunk - 1:chunk]
        ys.append(jnp.dot(h_re.astype(BF16), cr_ref[n], preferred_element_type=F32)
                  + jnp.dot(h_im.astype(BF16), ci_ref[n], preferred_element_type=F32))
    y = jnp.concatenate(ys, axis=1) + u * d_ref[...]
    z_ref[0] = _gelu_tanh(y).astype(BF16)


def _ssm_seq_call(x, mod, h0_re, h0_im, sp, d_skip):
    n, t, d = x.shape
    gp = sp["a_re"].shape[1]
    chunk = SSM_CHUNK
    assert t % chunk == 0
    tri = jnp.tril(jnp.ones((chunk, chunk), F32)).astype(BF16)
    tri2 = jnp.concatenate([tri, tri], axis=1)
    const = lambda a: pl.BlockSpec(a.shape, lambda i, j: tuple(0 for _ in a.shape))
    consts = [sp["a_re"], sp["a_im"], sp["pneg_re"], sp["pneg_im"], sp["ppos_re"], sp["ppos_im"],
              sp["bb_re"], sp["bb_im"], sp["c_re"], sp["c_im"], d_skip.reshape(1, d), tri2]
    state = pl.BlockSpec((1, 1, gp), lambda i, j: (i, 0, 0))
    const_bytes = sum(a.size * a.dtype.itemsize for a in consts)
    z, h_re, h_im = pl.pallas_call(
        _ssm_seq_kernel,
        grid=(n, t // chunk),
        in_specs=[pl.BlockSpec((1, chunk, d), lambda i, j: (i, j, 0)),
                  pl.BlockSpec((1, 6, 1, d), lambda i, j: (i, 0, 0, 0)), state, state]
                 + [const(a) for a in consts],
        out_specs=[pl.BlockSpec((1, chunk, d), lambda i, j: (i, j, 0)), state, state],
        out_shape=[jax.ShapeDtypeStruct((n, t, d), BF16),
                   jax.ShapeDtypeStruct((n, 1, gp), F32), jax.ShapeDtypeStruct((n, 1, gp), F32)],
        name="ssm_seq",
        compiler_params=_cparams(("parallel", "arbitrary"), 2 * const_bytes + 40 * chunk * gp),
    )(x, mod, h0_re.reshape(n, 1, gp), h0_im.reshape(n, 1, gp), *consts)
    return z, h_re.reshape(n, gp), h_im.reshape(n, gp)


def _ssm_step_kernel(x_ref, mod_ref, h0r_ref, h0i_ref, ar_ref, ai_ref, bbr_ref, bbi_ref, cr_ref, ci_ref, d_ref,
                     z_ref, hr_ref, hi_ref):
    u = x_ref[0] * (1.0 + mod_ref[0, 1]) + mod_ref[0, 0]
    ub = u.astype(BF16)
    nb, kin, kst = bbr_ref.shape
    ys = []
    for n in range(nb):
        cs = slice(n * kst, (n + 1) * kst)
        ubn = ub[:, n * kin:(n + 1) * kin]
        c_re, c_im = _cmul(ar_ref[:, cs], ai_ref[:, cs], h0r_ref[:, cs], h0i_ref[:, cs])
        h_re = jnp.dot(ubn, bbr_ref[n], preferred_element_type=F32) + c_re
        h_im = jnp.dot(ubn, bbi_ref[n], preferred_element_type=F32) + c_im
        hr_ref[:, cs] = h_re
        hi_ref[:, cs] = h_im
        ys.append(jnp.dot(h_re.astype(BF16), cr_ref[n], preferred_element_type=F32)
                  + jnp.dot(h_im.astype(BF16), ci_ref[n], preferred_element_type=F32))
    y = jnp.concatenate(ys, axis=1) + u * d_ref[...]
    z_ref[0] = _gelu_tanh(y).astype(BF16)


def _ssm_step_call(x, mod, h0_re, h0_im, sp, d_skip):
    _, rows, d = x.shape
    gp = sp["a_re"].shape[1]
    args = [x, mod, h0_re, h0_im, sp["a_re"], sp["a_im"], sp["bb_re"], sp["bb_im"], sp["c_re"], sp["c_im"],
            d_skip.reshape(1, d)]
    full = lambda a: pl.BlockSpec(a.shape, lambda: tuple(0 for _ in a.shape))
    out_shapes = [jax.ShapeDtypeStruct((1, rows, d), BF16),
                  jax.ShapeDtypeStruct((rows, gp), F32), jax.ShapeDtypeStruct((rows, gp), F32)]
    return pl.pallas_call(
        _ssm_step_kernel,
        in_specs=[full(a) for a in args],
        out_specs=[full(s) for s in out_shapes],
        out_shape=out_shapes,
        name="ssm_step",
        compiler_params=_cparams(None, 4 * sum(a.size * a.dtype.itemsize for a in args)),
    )(*args)


def _mixer_post_kernel(a_ref, x_ref, mod_ref, w_ref, lng_ref, lnb_ref, *rest, glu, router, alpha):
    if router:
        wr_ref, br_ref, x1_ref, h2_ref, lg_ref = rest
    else:
        x1_ref, h2_ref = rest
    proj = jnp.dot(a_ref[0], w_ref[...], preferred_element_type=F32)
    if glu:
        d = proj.shape[1] // 2
        proj = proj[:, :d] * _sigmoid(proj[:, d:])
    x1 = _post_norm(x_ref[0], proj, mod_ref[0, 2], lng_ref[...], lnb_ref[...], alpha)
    x1_ref[0] = x1
    h2 = x1 * (1.0 + mod_ref[0, 4]) + mod_ref[0, 3]
    h2_ref[0] = h2.astype(BF16)
    if router:
        nt = (((1,), (1,)), ((), ()))
        h_hi, h_lo = _split_bf16(h2, 2)
        w_hi, w_lo = _split_bf16(wr_ref[...], 2)
        lg = (lax.dot_general(w_hi, h_hi, nt, preferred_element_type=F32)
              + lax.dot_general(w_lo, h_hi, nt, preferred_element_type=F32)
              + lax.dot_general(w_hi, h_lo, nt, preferred_element_type=F32))
        lg_ref[...] = lg + br_ref[...]


def _mixer_post_call(a, x, mod, w, ln_g, ln_b, alpha, *, glu, w_router=None, b_router=None):
    n, t, d = x.shape
    tm = _row_tile(t, ROW_TILE)
    rmod = mod.shape[2]
    router = w_router is not None
    row = lambda width: pl.BlockSpec((1, tm, width), lambda i, j: (i, j, 0))
    const = lambda arr: pl.BlockSpec(arr.shape, lambda i, j: tuple(0 for _ in arr.shape))
    mod_spec = pl.BlockSpec((1, 6, rmod, d), (lambda i, j: (i, 0, 0, 0)) if rmod == 1 else (lambda i, j: (i, 0, j, 0)))
    args = [a, x, mod, w, ln_g.reshape(1, d), ln_b.reshape(1, d)]
    in_specs = [row(a.shape[2]), row(d), mod_spec, const(w), const(args[4]), const(args[5])]
    out_specs = [row(d), row(d)]
    out_shape = [jax.ShapeDtypeStruct((n, t, d), F32), jax.ShapeDtypeStruct((n, t, d), BF16)]
    if router:
        e = w_router.shape[1]
        args += [w_router.T, b_router.reshape(e, 1)]
        in_specs += [const(args[-2]), const(args[-1])]
        out_specs.append(pl.BlockSpec((e, tm), lambda i, j: (0, i * (t // tm) + j)))
        out_shape.append(jax.ShapeDtypeStruct((e, n * t), F32))
    vmem = 2 * w.size * 2 + 12 * tm * w.shape[1] * 4 + 8 * tm * d * 4
    return pl.pallas_call(
        functools.partial(_mixer_post_kernel, glu=glu, router=router, alpha=alpha),
        grid=(n, t // tm), in_specs=in_specs, out_specs=out_specs, out_shape=out_shape,
        name="mixer_post",
        compiler_params=_cparams(("parallel", "parallel"), vmem),
    )(*args)


def _swiglu_chunks(h, wg_ref, wu_ref, wd_ref, lead, fc):
    f = wg_ref.shape[-1]
    acc = None
    for lo in range(0, f, fc):
        hi = min(lo + fc, f)
        g = jnp.dot(h, wg_ref[lead + (slice(None), slice(lo, hi))], preferred_element_type=F32)
        u = jnp.dot(h, wu_ref[lead + (slice(None), slice(lo, hi))], preferred_element_type=F32)
        a = (g * _sigmoid(g) * u).astype(BF16)
        part = jnp.dot(a, wd_ref[lead + (slice(lo, hi), slice(None))], preferred_element_type=F32)
        acc = part if acc is None else acc + part
    return acc


def _ffn_post_kernel(h_ref, x_ref, mod_ref, wg_ref, wu_ref, wd_ref, lng_ref, lnb_ref, o_ref, *, alpha, fc):
    out = _swiglu_chunks(h_ref[0], wg_ref, wu_ref, wd_ref, (), fc)
    o_ref[0] = _post_norm(x_ref[0], out, mod_ref[0, 5], lng_ref[...], lnb_ref[...], alpha)


def _ffn_post_call(h, x, mod, wg, wu, wd, ln_g, ln_b, alpha):
    n, t, d = x.shape
    f = wg.shape[1]
    tm = _row_tile(t, ROW_TILE)
    rmod = mod.shape[2]
    row = lambda: pl.BlockSpec((1, tm, d), lambda i, j: (i, j, 0))
    const = lambda arr: pl.BlockSpec(arr.shape, lambda i, j: tuple(0 for _ in arr.shape),
                                     pipeline_mode=pl.Buffered(1))
    mod_spec = pl.BlockSpec((1, 6, rmod, d), (lambda i, j: (i, 0, 0, 0)) if rmod == 1 else (lambda i, j: (i, 0, j, 0)))
    fc = 2 * V7X_MXU_DIM
    args = [h, x, mod, wg, wu, wd, ln_g.reshape(1, d), ln_b.reshape(1, d)]
    vmem = 3 * d * f * 2 + 10 * tm * d * 4 + 6 * tm * fc * 4
    return pl.pallas_call(
        functools.partial(_ffn_post_kernel, alpha=alpha, fc=fc),
        grid=(n, t // tm),
        in_specs=[row(), row(), mod_spec] + [const(a) for a in args[3:]],
        out_specs=row(),
        out_shape=jax.ShapeDtypeStruct((n, t, d), F32), name="ffn_post",
        compiler_params=_cparams(("parallel", "parallel"), vmem),
    )(*args)


MOE_DISPATCH_TILE = 512
MOE_ROW_TILE = 512
MOE_COMBINE_TILE = 128
BF16_ROWS = 16


def _route_kernel(lg_ref, upper_ref, gate_ref, sel_ref, rank_ref, cnt_ref):
    @pl.when(pl.program_id(0) == 0)
    def _():
        cnt_ref[...] = jnp.zeros_like(cnt_ref)

    lg = lg_ref[...]
    e = lg.shape[0]
    idx = lax.broadcasted_iota(jnp.int32, lg.shape, 0)
    m1 = jnp.max(lg, axis=0, keepdims=True)
    i1 = jnp.min(jnp.where(lg == m1, idx, e), axis=0, keepdims=True)
    rest = jnp.where(idx == i1, -jnp.inf, lg)
    m2 = jnp.max(rest, axis=0, keepdims=True)
    i2 = jnp.min(jnp.where(rest == m2, idx, e), axis=0, keepdims=True)
    e2 = jnp.exp(m2 - m1)
    den = 1.0 + e2
    gate_ref[...] = jnp.where(idx == i1, 1.0 / den, 0.0) + jnp.where(idx == i2, e2 / den, 0.0)
    sel = jnp.where(idx == i1, 1.0, 0.0) + jnp.where(idx == i2, 1.0, 0.0)
    sel_ref[...] = sel
    rank_ref[...] = cnt_ref[...] + jnp.dot(sel.astype(BF16), upper_ref[...], preferred_element_type=F32)
    cnt_ref[...] += jnp.sum(sel, axis=1, keepdims=True)


def _route_call(logits_t):
    e, m = logits_t.shape
    tr = _row_tile(m, ROW_TILE)
    upper = jnp.triu(jnp.ones((tr, tr), F32), k=1).astype(BF16)
    blk = pl.BlockSpec((e, tr), lambda i: (0, i))
    out_shape = [jax.ShapeDtypeStruct((e, m), F32)] * 3 + [jax.ShapeDtypeStruct((e, 1), F32)]
    return pl.pallas_call(
        _route_kernel, grid=(m // tr,),
        in_specs=[blk, pl.BlockSpec((tr, tr), lambda i: (0, 0))],
        out_specs=[blk, blk, blk, pl.BlockSpec((e, 1), lambda i: (0, 0))],
        out_shape=out_shape, name="moe_route",
        compiler_params=_cparams(("arbitrary",), 4 * tr * tr * 2 + 64 * e * tr * 4),
    )(logits_t, upper)


def _moe_schedule(gate_t, sel_t, rank_t, cnt, tt, tme, tc):
    e, m = sel_t.shape
    cnt_i = cnt[:, 0].astype(jnp.int32)
    seg_tiles = cnt_i // tme + 1
    seg_end = jnp.cumsum(seg_tiles)
    seg_start = seg_end - seg_tiles
    offsets = seg_start * tme
    n_tiles = (TOP_K * m) // tme + e
    tile_ids = jnp.arange(n_tiles, dtype=jnp.int32)
    tile_expert = jnp.minimum(jnp.sum(tile_ids[:, None] >= seg_end[None, :], axis=1), e - 1).astype(jnp.int32)
    tile_valid = (tile_ids < seg_end[-1]).astype(jnp.int32)
    rank_i = rank_t.astype(jnp.int32)
    rank_at = jnp.concatenate([rank_i[:, ::tt], cnt_i[:, None]], axis=1)
    done_tiles = rank_at // tme
    w0 = (rank_at - done_tiles * tme) // BF16_ROWS * BF16_ROWS
    completes = jnp.concatenate([rank_at[:, 1:] >= (done_tiles[:, :-1] + 1) * tme,
                                 jnp.ones((e, 1), bool)], axis=1).astype(jnp.int32)
    pos = jnp.where(sel_t > 0, offsets[:, None].astype(F32) + rank_t, -1.0).T
    blk0 = ((offsets[:, None] + rank_i[:, ::tc]) // tc).T
    return dict(base=(done_tiles * tme + w0).reshape(-1), w0=w0.reshape(-1), completes=completes.reshape(-1),
                out_tile=(seg_start[:, None] + done_tiles).reshape(-1), tile_expert=tile_expert,
                tile_valid=tile_valid, n_tiles=n_tiles, blk0=blk0.reshape(-1).astype(jnp.int32), pos=pos,
                gates=gate_t.T)


def _dispatch_kernel(base_ref, w0_ref, done_ref, tile_ref, h_ref, sel_ref, rank_ref, o_ref, stage_ref, *, tme):
    del tile_ref
    ex = pl.program_id(0)
    j = pl.program_id(1)
    steps = pl.num_programs(1)
    flat = ex * steps + j
    tt = h_ref.shape[0]
    win = tt + BF16_ROWS

    @pl.when(j == 0)
    def _():
        stage_ref[...] = jnp.zeros_like(stage_ref)

    @pl.when(j < steps - 1)
    def _():
        w0 = pl.multiple_of(w0_ref[flat], BF16_ROWS)
        local = rank_ref[pl.ds(ex, 1), :] - base_ref[flat].astype(F32)
        local = jnp.where(sel_ref[pl.ds(ex, 1), :] > 0, local, -1.0)
        rows = lax.broadcasted_iota(jnp.int32, (win, tt), 0).astype(F32)
        onehot = jnp.where(rows == local, 1.0, 0.0).astype(BF16)
        stage_ref[pl.ds(w0, win), :] += jnp.dot(onehot, h_ref[...], preferred_element_type=F32)

    @pl.when(done_ref[flat] == 1)
    def _():
        o_ref[...] = stage_ref[0:tme, :].astype(BF16)
        tail = stage_ref[tme:, :]
        stage_ref[...] = jnp.zeros_like(stage_ref)
        stage_ref[0:win, :] = tail


def _dispatch_call(h, sel_t, rank_t, sched, tt, tme):
    m, d = h.shape
    e = sel_t.shape[0]
    nj = m // tt
    steps = nj + 1
    clamp = lambda j: jnp.minimum(j, nj - 1)
    grid_spec = pltpu.PrefetchScalarGridSpec(
        num_scalar_prefetch=4, grid=(e, steps),
        in_specs=[pl.BlockSpec((tt, d), lambda ex, j, *_: (clamp(j), 0)),
                  pl.BlockSpec((e, tt), lambda ex, j, *_: (0, clamp(j))),
                  pl.BlockSpec((e, tt), lambda ex, j, *_: (0, clamp(j)))],
        out_specs=pl.BlockSpec((tme, d), lambda ex, j, base, w0, done, tile: (tile[ex * steps + j], 0)),
        scratch_shapes=[pltpu.VMEM((tme + tt + BF16_ROWS, d), F32)])
    return pl.pallas_call(
        functools.partial(_dispatch_kernel, tme=tme), grid_spec=grid_spec,
        out_shape=jax.ShapeDtypeStruct((sched["n_tiles"] * tme, d), BF16), name="moe_dispatch",
        compiler_params=_cparams(("arbitrary", "arbitrary"), 12 * (tme + tt) * d * 4),
    )(sched["base"], sched["w0"], sched["completes"], sched["out_tile"], h, sel_t, rank_t)


def _group_ffn_kernel(texp_ref, valid_ref, x_ref, wg_ref, wu_ref, wd_ref, y_ref, *, fc):
    del texp_ref
    live = valid_ref[pl.program_id(0)] == 1

    @pl.when(live)
    def _():
        y_ref[...] = _swiglu_chunks(x_ref[...], wg_ref, wu_ref, wd_ref, (0,), fc).astype(BF16)

    @pl.when(jnp.logical_not(live))
    def _():
        y_ref[...] = jnp.zeros_like(y_ref)


def _group_ffn_call(xs, sched, wg, wu, wd, tme):
    rows, d = xs.shape
    f = wg.shape[2]
    fc = 2 * V7X_MXU_DIM
    row = pl.BlockSpec((tme, d), lambda i, texp, valid: (i, 0))
    grid_spec = pltpu.PrefetchScalarGridSpec(
        num_scalar_prefetch=2, grid=(rows // tme,),
        in_specs=[row,
                  pl.BlockSpec((1, d, f), lambda i, texp, valid: (texp[i], 0, 0)),
                  pl.BlockSpec((1, d, f), lambda i, texp, valid: (texp[i], 0, 0)),
                  pl.BlockSpec((1, f, d), lambda i, texp, valid: (texp[i], 0, 0))],
        out_specs=row)
    return pl.pallas_call(
        functools.partial(_group_ffn_kernel, fc=fc), grid_spec=grid_spec,
        out_shape=jax.ShapeDtypeStruct((rows, d), BF16), name="moe_group_ffn",
        compiler_params=_cparams(("arbitrary",), 2 * 3 * d * f * 2 + 8 * tme * d * 4 + 6 * tme * fc * 4),
    )(sched["tile_expert"], sched["tile_valid"], xs, wg, wu, wd)


def _combine_post_kernel(blk_ref, x_ref, mod_ref, pos_ref, gate_ref, lng_ref, lnb_ref, *rest, experts, alpha):
    y_refs, o_ref = rest[:2 * experts], rest[2 * experts]
    tc = x_ref.shape[1]
    tile = pl.program_id(0) * pl.num_programs(1) + pl.program_id(1)
    lane = lax.broadcasted_iota(jnp.int32, (tc, 2 * tc), 1).astype(F32)
    pos = pos_ref[...]
    gates = gate_ref[...]
    acc = jnp.zeros(x_ref.shape[1:], F32)
    for e in range(experts):
        base = (blk_ref[tile * experts + e] * tc).astype(F32)
        onehot = jnp.where(lane == pos[:, e:e + 1] - base, 1.0, 0.0).astype(BF16)
        window = jnp.concatenate([y_refs[2 * e][...], y_refs[2 * e + 1][...]], axis=0)
        acc = acc + gates[:, e:e + 1] * jnp.dot(onehot, window, preferred_element_type=F32)
    o_ref[0] = _post_norm(x_ref[0], acc, mod_ref[0, 5], lng_ref[...], lnb_ref[...], alpha)


def _combine_post_call(ys, x, mod, sched, ln_g, ln_b, alpha, tc):
    n, t, d = x.shape
    e = sched["pos"].shape[1]
    nj = t // tc
    nblk = ys.shape[0] // tc
    rmod = mod.shape[2]
    row = pl.BlockSpec((1, tc, d), lambda i, j, blk: (i, j, 0))
    per_tok = pl.BlockSpec((tc, e), lambda i, j, blk: (i * nj + j, 0))
    const = lambda arr: pl.BlockSpec(arr.shape, lambda i, j, blk: tuple(0 for _ in arr.shape))
    mod_spec = pl.BlockSpec((1, 6, rmod, d),
                            (lambda i, j, blk: (i, 0, 0, 0)) if rmod == 1 else (lambda i, j, blk: (i, 0, j, 0)))

    def gather_spec(ex, half):
        return pl.BlockSpec((tc, d), lambda i, j, blk: (jnp.minimum(blk[(i * nj + j) * e + ex] + half, nblk - 1), 0))

    lng, lnb = ln_g.reshape(1, d), ln_b.reshape(1, d)
    grid_spec = pltpu.PrefetchScalarGridSpec(
        num_scalar_prefetch=1, grid=(n, nj),
        in_specs=[row, mod_spec, per_tok, per_tok, const(lng), const(lnb)]
                 + [gather_spec(ex, half) for ex in range(e) for half in range(2)],
        out_specs=row)
    return pl.pallas_call(
        functools.partial(_combine_post_kernel, experts=e, alpha=alpha), grid_spec=grid_spec,
        out_shape=jax.ShapeDtypeStruct((n, t, d), F32), name="moe_combine_post",
        compiler_params=_cparams(("parallel", "parallel"), 4 * 2 * e * tc * d * 2 + 16 * tc * d * 4),
    )(sched["blk0"], x, mod, sched["pos"], sched["gates"], lng, lnb, *([ys] * (2 * e)))


def _moe_call(h2, logits_t, x1, mod, wg, wu, wd, ln_g, ln_b, alpha):
    n, t, d = x1.shape
    m = n * t
    tt = min(MOE_DISPATCH_TILE, m)
    tme = min(MOE_ROW_TILE, m)
    tc = min(MOE_COMBINE_TILE, t)
    gate_t, sel_t, rank_t, cnt = _route_call(logits_t)
    sched = _moe_schedule(gate_t, sel_t, rank_t, cnt, tt, tme, tc)
    xs = _dispatch_call(h2.reshape(m, d), sel_t, rank_t, sched, tt, tme)
    ys = _group_ffn_call(xs, sched, wg, wu, wd, tme)
    return _combine_post_call(ys, x1, mod, sched, ln_g, ln_b, alpha, tc)


def _log_sigmoid(z):
    return -(jnp.maximum(-z, 0.0) + jnp.log1p(jnp.exp(-jnp.abs(z))))


ATTN_LANES = V7X_LANES
AUG_TERMS = 3
LOG2E = 1.4426950408889634


def _aug_selectors(heads, hd, offset):
    sel = []
    for i in range(AUG_TERMS):
        col = jnp.arange(heads) * ATTN_LANES + hd + offset + i
        sel.append(jnp.zeros((heads, heads * ATTN_LANES), F32).at[jnp.arange(heads), col].set(1.0).astype(BF16))
    return jnp.stack(sel)


def _aug_ones(heads, hd, offset, count):
    lane = jnp.arange(heads * ATTN_LANES) % ATTN_LANES
    return ((lane >= hd + offset) & (lane < hd + offset + count)).astype(F32).reshape(1, heads * ATTN_LANES)


def _head_major(w, heads):
    d, dh = w.shape
    hd = dh // heads
    return jnp.pad(w.reshape(d, heads, hd), ((0, 0), (0, 0), (0, ATTN_LANES - hd))).reshape(d, heads * ATTN_LANES)


def _place_terms(x, sel_ref, sign):
    out = None
    for i, term in enumerate(_split_bf16(x, AUG_TERMS)):
        part = jnp.dot(term, sel_ref[i], preferred_element_type=F32)
        out = part if out is None else out + part
    return sign * out


def _kv_kernel(x_ref, wk_ref, wv_ref, wf_ref, bf_ref, *rest, cumulative):
    if cumulative:
        (tri_ref, wka_ref, wva_ref, sel_ref, kone_ref, vone_ref,
         k_ref, v_ref, lf_ref, fk_ref, ka_ref, va_ref, carry_ref) = rest
    else:
        k_ref, v_ref, lf_ref = rest
    xb = x_ref[0].astype(BF16)
    k_ref[0] = jnp.dot(xb, wk_ref[...], preferred_element_type=F32)
    v_ref[0] = jnp.dot(xb, wv_ref[...], preferred_element_type=F32)
    lf = _log_sigmoid(jnp.dot(xb, wf_ref[...], preferred_element_type=F32) + bf_ref[...])
    lf_ref[0] = lf
    if cumulative:
        @pl.when(pl.program_id(1) == 0)
        def _():
            carry_ref[...] = jnp.zeros_like(carry_ref)

        fk = carry_ref[...] + jnp.dot(tri_ref[...], jnp.concatenate(_split_bf16(lf, 3), axis=0),
                                      preferred_element_type=F32)
        fk_ref[0] = fk
        carry_ref[...] = fk[fk.shape[0] - 1:]
        ka = (jnp.dot(xb, wka_ref[...], preferred_element_type=F32) + kone_ref[...]
              + _place_terms(fk * LOG2E, sel_ref, -1.0)).astype(BF16)
        va = (jnp.dot(xb, wva_ref[...], preferred_element_type=F32) + vone_ref[...]).astype(BF16)
        for h in range(ka_ref.shape[1]):
            ka_ref[0, h] = ka[:, h * ATTN_LANES:(h + 1) * ATTN_LANES]
            va_ref[0, h] = va[:, h * ATTN_LANES:(h + 1) * ATTN_LANES]


def _kv_call(x, wk, wv, wf, bf, *, cumulative):
    n, t, d = x.shape
    hh = wf.shape[1]
    hd = d // hh
    tm = _row_tile(t, ROW_TILE)
    row = lambda width: pl.BlockSpec((1, tm, width), lambda i, j: (i, j, 0))
    const = lambda arr: pl.BlockSpec(arr.shape, lambda i, j: tuple(0 for _ in arr.shape))
    args = [x, wk, wv, wf, bf.reshape(1, hh)]
    out_specs = [row(d), row(d), row(hh)]
    out_shape = [jax.ShapeDtypeStruct((n, t, d), F32)] * 2 + [jax.ShapeDtypeStruct((n, t, hh), F32)]
    scratch = []
    if cumulative:
        tri = jnp.tril(jnp.ones((tm, tm), F32)).astype(BF16)
        args += [jnp.concatenate([tri] * 3, axis=1), _head_major(wk, hh), _head_major(wv, hh),
                 _aug_selectors(hh, hd, AUG_TERMS), _aug_ones(hh, hd, 0, AUG_TERMS), _aug_ones(hh, hd, 0, 1)]
        head_rows = pl.BlockSpec((1, hh, tm, ATTN_LANES), lambda i, j: (i, 0, j, 0))
        out_specs += [row(hh), head_rows, head_rows]
        out_shape += [jax.ShapeDtypeStruct((n, t, hh), F32)] + [jax.ShapeDtypeStruct((n, hh, t, ATTN_LANES), BF16)] * 2
        scratch = [pltpu.VMEM((1, hh), F32)]
    return pl.pallas_call(
        functools.partial(_kv_kernel, cumulative=cumulative),
        grid=(n, t // tm),
        in_specs=[row(d)] + [const(a) for a in args[1:]],
        out_specs=out_specs, out_shape=out_shape, scratch_shapes=scratch, name="kv_proj",
        compiler_params=_cparams(("parallel", "arbitrary"), 16 * d * d + 40 * tm * d * 4),
    )(*args)


def _q_kernel(x_ref, mod_ref, w_ref, *rest, scale, augmented):
    h = (x_ref[0] * (1.0 + mod_ref[0, 1]) + mod_ref[0, 0]).astype(BF16)
    q = jnp.dot(h, w_ref[...], preferred_element_type=F32) * scale
    if augmented:
        fq_ref, sel_ref, one_ref, q_ref = rest
        qa = (q + one_ref[...] + _place_terms(fq_ref[0] * LOG2E, sel_ref, 1.0)).astype(BF16)
        for hh in range(q_ref.shape[1]):
            q_ref[0, hh] = qa[:, hh * ATTN_LANES:(hh + 1) * ATTN_LANES]
    else:
        q_ref, = rest
        q_ref[0] = q.astype(BF16)


def _q_call(x, mod, wq, heads, fq=None):
    n, t, d = x.shape
    hd = d // heads
    tm = _row_tile(t, ROW_TILE)
    rmod = mod.shape[2]
    row = lambda width: pl.BlockSpec((1, tm, width), lambda i, j: (i, j, 0))
    const = lambda arr: pl.BlockSpec(arr.shape, lambda i, j: tuple(0 for _ in arr.shape))
    mod_spec = pl.BlockSpec((1, 6, rmod, d), (lambda i, j: (i, 0, 0, 0)) if rmod == 1 else (lambda i, j: (i, 0, j, 0)))
    augmented = fq is not None
    if augmented:
        args = [x, mod, _head_major(wq, heads), fq, _aug_selectors(heads, hd, 0),
                _aug_ones(heads, hd, AUG_TERMS, AUG_TERMS)]
        in_specs = [row(d), mod_spec, const(args[2]), row(heads), const(args[4]), const(args[5])]
        out_specs = pl.BlockSpec((1, heads, tm, ATTN_LANES), lambda i, j: (i, 0, j, 0))
        out_shape = jax.ShapeDtypeStruct((n, heads, t, ATTN_LANES), BF16)
        scale = hd ** -0.5 * LOG2E
    else:
        args = [x, mod, wq]
        in_specs = [row(d), mod_spec, const(wq)]
        out_specs = row(d)
        out_shape = jax.ShapeDtypeStruct((n, t, d), BF16)
        scale = hd ** -0.5
    return pl.pallas_call(
        functools.partial(_q_kernel, scale=scale, augmented=augmented),
        grid=(n, t // tm), in_specs=in_specs, out_specs=out_specs, out_shape=out_shape, name="q_proj",
        compiler_params=_cparams(("parallel", "parallel"), 8 * d * d + 24 * tm * d * 4),
    )(*args)


ATTN_ROW_BLOCK = 128
ATTN_HEAD_GROUP = 8


def _attn_kernel(q_ref, k_ref, v_ref, o_ref, m_ref, acc_ref, *, hd):
    qi = pl.program_id(1)
    kj = pl.program_id(2)
    heads, tq = q_ref.shape[1], q_ref.shape[2]
    tk = k_ref.shape[2]
    rb = min(ATTN_ROW_BLOCK, tq)
    group = math.gcd(heads, ATTN_HEAD_GROUP)
    nt = (((1,), (1,)), ((), ()))

    @pl.when(kj == 0)
    def _():
        m_ref[...] = jnp.full_like(m_ref, NEG_INF)
        acc_ref[...] = jnp.zeros_like(acc_ref)

    def sweep(diagonal):
        blocks = [slice(r * rb, (r + 1) * rb) for r in range(tq // rb)]

        def group_body(g, carry):
            hs = [g * group + i for i in range(group)]
            scores = [[lax.dot_general(q_ref[0, h, rows, :], k_ref[0, h], nt, preferred_element_type=F32)
                       for rows in blocks] for h in hs]
            for h, s_h in zip(hs, scores):
                vh = v_ref[0, h]
                m_all = m_ref[h]
                acc_all = acc_ref[h]
                m_out, acc_out = [], []
                for r, (rows, s) in enumerate(zip(blocks, s_h)):
                    if diagonal:
                        qpos = r * rb + lax.broadcasted_iota(jnp.int32, (rb, tk), 0)
                        kpos = lax.broadcasted_iota(jnp.int32, (rb, tk), 1)
                        s = jnp.where(kpos <= qpos, s, NEG_INF)
                    m_prev = m_all[rows]
                    m_new = jnp.maximum(m_prev, jnp.max(s, axis=-1, keepdims=True))
                    p = jnp.exp2(s - m_new)
                    acc_out.append(jnp.exp2(m_prev - m_new) * acc_all[rows]
                                   + jnp.dot(p.astype(BF16), vh, preferred_element_type=F32))
                    m_out.append(m_new)
                m_ref[h] = jnp.concatenate(m_out, axis=0)
                acc_ref[h] = jnp.concatenate(acc_out, axis=0)
            return carry
        lax.fori_loop(0, heads // group, group_body, 0)

    @pl.when(kj < qi)
    def _():
        sweep(False)

    @pl.when(kj == qi)
    def _():
        sweep(True)
        for h in range(heads):
            a = acc_ref[h]
            o_ref[0, :, h * hd:(h + 1) * hd] = (a[:, :hd] / a[:, hd:hd + 1]).astype(BF16)


def _attn_call(qa, ka, va, hd):
    n, heads, t, lanes = qa.shape
    tq = tk = _row_tile(t, ATTN_TILE)
    kv_spec = pl.BlockSpec((1, heads, tk, lanes), lambda i, a, b: (i, 0, jnp.minimum(a, b), 0))
    return pl.pallas_call(
        functools.partial(_attn_kernel, hd=hd),
        grid=(n, t // tq, t // tk),
        in_specs=[pl.BlockSpec((1, heads, tq, lanes), lambda i, a, b: (i, 0, a, 0)), kv_spec, kv_spec],
        out_specs=pl.BlockSpec((1, tq, heads * hd), lambda i, a, b: (i, a, 0)),
        out_shape=jax.ShapeDtypeStruct((n, t, heads * hd), BF16),
        scratch_shapes=[pltpu.VMEM((heads, tq, 1), F32), pltpu.VMEM((heads, tq, lanes), F32)],
        name="causal_attn",
        compiler_params=_cparams(("parallel", "parallel", "arbitrary"),
                                 2 * heads * tq * lanes * 4 + 8 * heads * tq * lanes * 2 + 16 * tq * tk * 4),
    )(qa, ka, va)


def _column_to_row(col, eye):
    return jnp.sum(jnp.where(eye > 0, col, 0.0), axis=0, keepdims=True)


DECODE_PAGES_PER_STEP = 8


def _decode_attn_kernel(pt_ref, qb_ref, *rest, g):
    del pt_ref
    kt_refs, vt_refs, lft_refs = rest[:g], rest[g:2 * g], rest[2 * g:3 * g]
    (knew_ref, vnew_ref, lfnew_ref, later_ref, diag_ref, eye_ref, o_ref, m_ref, l_ref, acc_ref, suf_ref) = rest[3 * g:]
    step = pl.program_id(1)
    qb = qb_ref[0]
    eye = eye_ref[...]

    @pl.when(step == 0)
    def _():
        m_ref[...] = jnp.dot(qb, knew_ref[0].astype(BF16), preferred_element_type=F32)
        l_ref[...] = jnp.ones_like(l_ref)
        acc_ref[...] = jnp.broadcast_to(vnew_ref[0], acc_ref.shape)
        suf_ref[...] = lfnew_ref[0]

    scores = []
    after = suf_ref[...]
    for kt_ref, lft_ref in zip(kt_refs, lft_refs):
        lf = lft_ref[0]
        lf3 = jnp.concatenate(_split_bf16(lf, 3), axis=1)
        scores.append(jnp.dot(qb, kt_ref[0].astype(BF16), preferred_element_type=F32)
                      + jnp.dot(lf3, later_ref[...], preferred_element_type=F32) + after)
        after = after + jnp.sum(lf, axis=1, keepdims=True)
    suf_ref[...] = after
    m_prev = m_ref[...]
    m_new = m_prev
    for s in scores:
        m_new = jnp.maximum(m_new, jnp.max(s, axis=1, keepdims=True))
    alpha = jnp.exp(m_prev - m_new)
    l_new = alpha * l_ref[...]
    acc = _column_to_row(alpha, eye) * acc_ref[...]
    for s, vt_ref in zip(scores, vt_refs):
        p = jnp.exp(s - m_new)
        l_new = l_new + jnp.sum(p, axis=1, keepdims=True)
        acc = acc + lax.dot_general(vt_ref[0].astype(BF16), p.astype(BF16), (((1,), (1,)), ((), ())),
                                    preferred_element_type=F32)
    l_ref[...] = l_new
    acc_ref[...] = acc
    m_ref[...] = m_new

    @pl.when(step == pl.num_programs(1) - 1)
    def _():
        on_diag = diag_ref[...] > 0
        num = jnp.sum(jnp.where(on_diag, acc_ref[...], 0.0), axis=1, keepdims=True)
        den = jnp.sum(jnp.where(on_diag, _column_to_row(l_ref[...], eye), 0.0), axis=1, keepdims=True)
        o_ref[0] = (num / den).astype(BF16)


def _decode_attn_call(page_table, q, cache_k, cache_v, cache_logf, k_new, v_new, lf_new, heads):
    nseq, npages = page_table.shape
    n_phys, page, _, hd = cache_k.shape
    d = heads * hd
    seg = jnp.repeat(jnp.eye(heads, dtype=F32), hd, axis=1)
    qb = q[:, None, :] * seg[None].astype(BF16)
    later = jnp.tril(jnp.ones((page, page), F32), k=-1).astype(BF16)
    later3 = jnp.concatenate([later] * 3, axis=0)
    kt = cache_k.transpose(0, 2, 3, 1).reshape(n_phys, d, page)
    vt = cache_v.transpose(0, 2, 3, 1).reshape(n_phys, d, page)
    lft = cache_logf.transpose(0, 2, 1)
    g = math.gcd(npages, DECODE_PAGES_PER_STEP)

    def paged(rows):
        return [pl.BlockSpec((1, rows, page), functools.partial(
            lambda b, p, pt, i: (pt[b, npages - 1 - (p * g + i)], 0, 0), i=i)) for i in range(g)]

    per_seq = lambda arr: pl.BlockSpec((1,) + arr.shape[1:], lambda b, p, pt: (b,) + tuple(0 for _ in arr.shape[1:]))
    const = lambda arr: pl.BlockSpec(arr.shape, lambda b, p, pt: tuple(0 for _ in arr.shape))
    tail = [k_new.reshape(nseq, d, 1), v_new.reshape(nseq, d, 1), lf_new.reshape(nseq, heads, 1),
            later3, seg.T, jnp.eye(heads, dtype=F32)]
    args = [qb] + [kt] * g + [vt] * g + [lft] * g + tail
    grid_spec = pltpu.PrefetchScalarGridSpec(
        num_scalar_prefetch=1, grid=(nseq, npages // g),
        in_specs=[per_seq(qb)] + paged(d) + paged(d) + paged(heads)
                 + [per_seq(a) for a in tail[:3]] + [const(a) for a in tail[3:]],
        out_specs=pl.BlockSpec((1, d, 1), lambda b, p, pt: (b, 0, 0)),
        scratch_shapes=[pltpu.VMEM((heads, 1), F32), pltpu.VMEM((heads, 1), F32), pltpu.VMEM((d, heads), F32),
                        pltpu.VMEM((heads, 1), F32)])
    out = pl.pallas_call(
        functools.partial(_decode_attn_kernel, g=g), grid_spec=grid_spec,
        out_shape=jax.ShapeDtypeStruct((nseq, d, 1), BF16), name="paged_decode_attn",
        compiler_params=_cparams(("parallel", "arbitrary"), (8 + 7 * g) * page * d * 4),
    )(page_table, *args)
    return out.reshape(1, nseq, d)


def _trunk(x, mods, h0_re, h0_im, wts, ssm, *, sequence, paged=None):
    depth = mods.shape[0]
    n_a = wts["w_glu"].shape[0]
    heads = wts["w_f"].shape[1]
    nb, rows, d = x.shape
    hd = d // heads
    alpha = (2.0 * depth) ** 0.25
    new_re, new_im = [], []
    k_new = v_new = lf_new = None
    fk = k_aug = v_aug = None
    for l in range(depth):
        mod = mods[l]
        moe = l % 2 == 1
        li = l // 2
        router = dict(w_router=wts["w_router"][li], b_router=wts["b_router"][li]) if moe else {}
        if l < n_a:
            if sequence:
                z, hr, hi = _ssm_seq_call(x, mod, h0_re[l], h0_im[l], ssm[l], wts["ssm_d"][l])
            else:
                z, hr, hi = _ssm_step_call(x, mod, h0_re[l], h0_im[l], ssm[l], wts["ssm_d"][l])
            new_re.append(hr)
            new_im.append(hi)
            res = _mixer_post_call(z, x, mod, wts["w_glu"][l], wts["ln_g"][l, 0], wts["ln_b"][l, 0], alpha,
                                   glu=True, **router)
        else:
            lb = l - n_a
            if sequence:
                o = _attn_call(_q_call(x, mod, wts["w_q"][lb], heads, fq=fk), k_aug, v_aug, hd)
            else:
                q = _q_call(x, mod, wts["w_q"][lb], heads)
                o = _decode_attn_call(paged[0], q[0], paged[1], paged[2], paged[3], k_new[0], v_new[0], lf_new[0],
                                      heads)
            res = _mixer_post_call(o, x, mod, wts["w_o"][lb], wts["ln_g"][l, 0], wts["ln_b"][l, 0], alpha,
                                   glu=False, **router)
        if moe:
            x1, h2, logits_t = res
            x = _moe_call(h2, logits_t, x1, mod, wts["w_exp_gate"][li], wts["w_exp_up"][li],
                          wts["w_exp_down"][li], wts["ln_g"][l, 1], wts["ln_b"][l, 1], alpha)
        else:
            x1, h2 = res
            x = _ffn_post_call(h2, x1, mod, wts["w_ff_gate"][li], wts["w_ff_up"][li], wts["w_ff_down"][li],
                               wts["ln_g"][l, 1], wts["ln_b"][l, 1], alpha)
        if l == n_a - 1:
            outs = _kv_call(x, wts["w_k"], wts["w_v"], wts["w_f"], wts["b_f"], cumulative=sequence)
            k_new, v_new, lf_new = outs[:3]
            if sequence:
                fk, k_aug, v_aug = outs[3:]
    return x, jnp.stack(new_re), jnp.stack(new_im), k_new, v_new, lf_new


def kernel(x_prompt, x_sample, c_prompt, c_sample, cache_k, cache_v, cache_logf, state_ssm_re, state_ssm_im,
           page_table, w_mod, b_mod, ln_g, ln_b, ssm_lam_re, ssm_lam_im, ssm_log_dt, ssm_b_re, ssm_b_im,
           ssm_c_re, ssm_c_im, ssm_d, w_glu, w_k, w_v, w_f, b_f, w_q, w_o, w_ff_gate, w_ff_up, w_ff_down,
           w_router, b_router, w_exp_gate, w_exp_up, w_exp_down):
    batch, seq, d = x_prompt.shape
    dec_batch, dec_seq, _ = x_sample.shape
    assert dec_seq == 1
    depth = w_mod.shape[0]
    n_a, g, p = ssm_lam_re.shape
    heads = w_f.shape[1]
    hd = d // heads
    gp = g * p

    n_c = batch + dec_batch
    c_all = jnp.concatenate([c_prompt, c_sample], axis=0)
    c_all = jnp.pad(c_all, ((0, (-n_c) % 8), (0, 0)))
    mods = _mod_call(c_all, w_mod, b_mod)
    mods_p = mods[:, :batch].reshape(depth, batch, 6, 1, d)
    mods_s = mods[:, batch:n_c].reshape(depth, dec_batch, 6, d).transpose(0, 2, 1, 3)[:, None]

    bf = lambda w: w.astype(BF16)
    wts = dict(ln_g=ln_g, ln_b=ln_b, ssm_d=ssm_d, w_glu=bf(w_glu), w_k=bf(w_k), w_v=bf(w_v), w_f=bf(w_f), b_f=b_f,
               w_q=bf(w_q), w_o=bf(w_o), w_ff_gate=bf(w_ff_gate), w_ff_up=bf(w_ff_up), w_ff_down=bf(w_ff_down),
               w_router=w_router, b_router=b_router, w_exp_gate=bf(w_exp_gate), w_exp_up=bf(w_exp_up),
               w_exp_down=bf(w_exp_down))
    ssm = [_ssm_prep(ssm_lam_re[l], ssm_lam_im[l], ssm_log_dt[l], ssm_b_re[l], ssm_b_im[l], ssm_c_re[l],
                     ssm_c_im[l]) for l in range(n_a)]

    h0 = jnp.zeros((n_a, batch, gp), F32)
    y_p, re_p, im_p, k_p, v_p, lf_p = _trunk(x_prompt, mods_p, h0, h0, wts, ssm, sequence=True)

    x_s = x_sample.reshape(1, dec_batch, d)
    y_s, re_s, im_s, k_s, v_s, lf_s = _trunk(
        x_s, mods_s, state_ssm_re.reshape(n_a, dec_batch, gp), state_ssm_im.reshape(n_a, dec_batch, gp), wts, ssm,
        sequence=False, paged=(page_table, cache_k, cache_v, cache_logf))

    return (y_p, y_s.reshape(dec_batch, 1, d),
            re_p.reshape(n_a, batch, g, p), im_p.reshape(n_a, batch, g, p),
            k_p.reshape(batch, seq, heads, hd), v_p.reshape(batch, seq, heads, hd), lf_p,
            re_s.reshape(n_a, dec_batch, g, p), im_s.reshape(n_a, dec_batch, g, p),
            k_s.reshape(dec_batch, 1, heads, hd), v_s.reshape(dec_batch, 1, heads, hd),
            lf_s.reshape(dec_batch, 1, heads))
```

```python
import functools
import math

import jax
import jax.numpy as jnp
from jax import lax
from jax.experimental import pallas as pl
from jax.experimental.pallas import tpu as pltpu

F32 = jnp.float32
BF16 = jnp.bfloat16

GROUP_SIZE = 16
TOP_K = 2
LN_EPS = 1e-5
NEG_INF = -1e30

V7X_MXU_DIM = 256
V7X_LANES = 128
V7X_VMEM_BUDGET = 56 * 1024 * 1024

SSM_CHUNK = 128
ROW_TILE = 512
ATTN_TILE = 512


def _cparams(semantics, vmem_bytes):
    return pltpu.CompilerParams(dimension_semantics=semantics,
                                vmem_limit_bytes=int(min(max(vmem_bytes, 16 * 1024 * 1024), V7X_VMEM_BUDGET)))


def _row_tile(t, pref):
    tile = min(t, pref)
    assert t % tile == 0, (t, tile)
    return tile


def _sigmoid(x):
    return 1.0 / (1.0 + jnp.exp(-x))


def _split_bf16(x, parts):
    out = []
    r = x
    for _ in range(parts):
        p = r.astype(BF16)
        out.append(p)
        r = r - p.astype(F32)
    return out


def _post_norm(x, out, gate, g, b, alpha):
    y = alpha * x + (1.0 + gate) * out
    mu = jnp.mean(y, axis=-1, keepdims=True)
    d = y - mu
    var = jnp.mean(d * d, axis=-1, keepdims=True)
    return d * lax.rsqrt(var + LN_EPS) * g + b


def _mod_kernel(c_ref, w_ref, b_ref, o_ref):
    c = c_ref[...]
    s = (c * _sigmoid(c)).astype(BF16)
    o_ref[0] = jnp.dot(s, w_ref[0].astype(BF16), preferred_element_type=F32) + b_ref[0]


def _mod_call(c_all, w_mod, b_mod):
    depth, d, d6 = w_mod.shape
    r = c_all.shape[0]
    tn = _row_tile(d6, 1536)
    return pl.pallas_call(
        _mod_kernel,
        grid=(depth, d6 // tn),
        in_specs=[pl.BlockSpec((r, d), lambda l, j: (0, 0)),
                  pl.BlockSpec((1, d, tn), lambda l, j: (l, 0, j)),
                  pl.BlockSpec((1, 1, tn), lambda l, j: (l, 0, j))],
        out_specs=pl.BlockSpec((1, r, tn), lambda l, j: (l, 0, j)),
        out_shape=jax.ShapeDtypeStruct((depth, r, d6), F32),
        name="adaln_mod",
        compiler_params=_cparams(("parallel", "parallel"), 3 * d * tn * 4),
    )(c_all, w_mod, b_mod.reshape(depth, 1, d6))


def _discretise(lr, li, ldt):
    dt = jnp.exp(ldt)
    mag = jnp.exp(lr * dt)
    a_re = mag * jnp.cos(li * dt)
    a_im = mag * jnp.sin(li * dt)
    den = lr * lr + li * li
    nr = a_re - 1.0
    k_re = (nr * lr + a_im * li) / den
    k_im = (a_im * lr - nr * li) / den
    return a_re, a_im, k_re, k_im


def _cmul(ar, ai, br, bi):
    return ar * br - ai * bi, ar * bi + ai * br


def _ssm_prep_kernel(lr_row, li_row, dt_row, lr_col, li_col, dt_col, b_re, b_im,
                     a_re_o, a_im_o, pneg_re_o, pneg_im_o, ppos_re_o, ppos_im_o, bb_re_o, bb_im_o):
    a_re, a_im, _, _ = _discretise(lr_row[...], li_row[...], dt_row[...])
    a_re_o[...] = a_re
    a_im_o[...] = a_im
    _, _, k_re, k_im = _discretise(lr_col[...], li_col[...], dt_col[...])
    br = b_re[...]
    bi = b_im[...]
    bb_re_o[...] = k_re * br - k_im * bi
    bb_im_o[...] = k_re * bi + k_im * br

    chunk = ppos_re_o.shape[0]
    t = lax.broadcasted_iota(jnp.int32, ppos_re_o.shape, 0)
    n2 = a_re * a_re + a_im * a_im
    for (sq_re, sq_im, o_re, o_im) in ((a_re, a_im, ppos_re_o, ppos_im_o),
                                       (a_re / n2, -a_im / n2, pneg_re_o, pneg_im_o)):
        p_re = jnp.ones(ppos_re_o.shape, F32)
        p_im = jnp.zeros(ppos_re_o.shape, F32)
        bit = 1
        while bit < chunk:
            on = (t & bit) != 0
            f_re = jnp.where(on, sq_re, 1.0)
            f_im = jnp.where(on, sq_im, 0.0)
            p_re, p_im = _cmul(p_re, p_im, f_re, f_im)
            sq_re, sq_im = _cmul(sq_re, sq_im, sq_re, sq_im)
            bit *= 2
        o_re[...] = p_re
        o_im[...] = p_im


def _ssm_prep(lam_re, lam_im, log_dt, b_re, b_im, c_re, c_im):
    g, p = lam_re.shape
    gs = b_re.shape[-1]
    gp = g * p
    dt = jnp.broadcast_to(log_dt[:, None], (g, p))
    rows = [a.reshape(1, gp) for a in (lam_re, lam_im, dt)]
    cols = [a.reshape(gp, 1) for a in (lam_re, lam_im, dt)]
    full = lambda shape: pl.BlockSpec(shape, lambda: tuple(0 for _ in shape))
    out_shapes = ([jax.ShapeDtypeStruct((1, gp), F32)] * 2 + [jax.ShapeDtypeStruct((SSM_CHUNK, gp), F32)] * 4
                  + [jax.ShapeDtypeStruct((gp, gs), F32)] * 2)
    a_re, a_im, pneg_re, pneg_im, ppos_re, ppos_im, bb_re, bb_im = pl.pallas_call(
        _ssm_prep_kernel,
        in_specs=[full((1, gp))] * 3 + [full((gp, 1))] * 3 + [full((gp, gs))] * 2,
        out_specs=[full(s.shape) for s in out_shapes],
        out_shape=out_shapes,
        name="ssm_prep",
        compiler_params=_cparams(None, 48 * SSM_CHUNK * gp * 4),
    )(*rows, *cols, b_re.reshape(gp, gs), b_im.reshape(gp, gs))

    gpb = V7X_MXU_DIM // gs
    nb = g // gpb
    eye = jnp.eye(gpb, dtype=F32)

    def in_proj(bb):
        t = bb.reshape(nb, gpb, p, gs).transpose(0, 1, 3, 2)
        return jnp.einsum("nghp,gk->nghkp", t, eye).reshape(nb, gpb * gs, gpb * p).astype(BF16)

    def out_proj(c):
        t = c.reshape(nb, gpb, gs, p)
        return jnp.einsum("nghp,gk->nkpgh", t, eye).reshape(nb, gpb * p, gpb * gs).astype(BF16)

    return dict(a_re=a_re, a_im=a_im, pneg_re=pneg_re, pneg_im=pneg_im, ppos_re=ppos_re, ppos_im=ppos_im,
                bb_re=in_proj(bb_re), bb_im=in_proj(bb_im), c_re=out_proj(c_re), c_im=out_proj(-c_im))


def _gelu_tanh(y):
    return 0.5 * y * (1.0 + jnp.tanh(0.7978845608028654 * (y + 0.044715 * (y * y * y))))


def _ssm_seq_kernel(x_ref, mod_ref, h0r_ref, h0i_ref, ar_ref, ai_ref, pnr_ref, pni_ref, ppr_ref, ppi_ref,
                    bbr_ref, bbi_ref, cr_ref, ci_ref, d_ref, tri_ref, z_ref, hr_ref, hi_ref):
    step = pl.program_id(1)

    @pl.when(step == 0)
    def _():
        hr_ref[0] = h0r_ref[0]
        hi_ref[0] = h0i_ref[0]

    chunk = x_ref.shape[1]
    u = x_ref[0] * (1.0 + mod_ref[0, 1]) + mod_ref[0, 0]
    ub = u.astype(BF16)
    nb, kin, kst = bbr_ref.shape
    row0 = lax.broadcasted_iota(jnp.int32, (chunk, kst), 0) == 0
    tri = tri_ref[...]
    cols = [slice(n * kst, (n + 1) * kst) for n in range(nb)]
    carry = [_cmul(ar_ref[:, cs], ai_ref[:, cs], hr_ref[0, :, cs], hi_ref[0, :, cs]) for cs in cols]
    xs = [(jnp.dot(ub[:, n * kin:(n + 1) * kin], bbr_ref[n], preferred_element_type=F32),
           jnp.dot(ub[:, n * kin:(n + 1) * kin], bbi_ref[n], preferred_element_type=F32)) for n in range(nb)]
    cums = []
    for cs, (x_re, x_im), (c_re, c_im) in zip(cols, xs, carry):
        x_re = x_re + jnp.where(row0, c_re, 0.0)
        x_im = x_im + jnp.where(row0, c_im, 0.0)
        scaled = _cmul(x_re, x_im, pnr_ref[:, cs], pni_ref[:, cs])
        cums.append([jnp.dot(tri, jnp.concatenate(_split_bf16(s, 2), axis=0), preferred_element_type=F32)
                     for s in scaled])
    ys = []
    for n, (cs, cum) in enumerate(zip(cols, cums)):
        h_re, h_im = _cmul(cum[0], cum[1], ppr_ref[:, cs], ppi_ref[:, cs])
        hr_ref[0, :, cs] = h_re[chunk - 1:chunk]
        hi_ref[0, :, cs] = h_im[chunk - 1:chunk]
        ys.append(jnp.dot(h_re.astype(BF16), cr_ref[n], preferred_element_type=F32)
                  + jnp.dot(h_im.astype(BF16), ci_ref[n], preferred_element_type=F32))
    y = jnp.concatenate(ys, axis=1) + u * d_ref[...]
    z_ref[0] = _gelu_tanh(y).astype(BF16)


def _ssm_seq_call(x, mod, h0_re, h0_im, sp, d_skip):
    n, t, d = x.shape
    gp = sp["a_re"].shape[1]
    chunk = SSM_CHUNK
    assert t % chunk == 0
    tri = jnp.tril(jnp.ones((chunk, chunk), F32)).astype(BF16)
    tri2 = jnp.concatenate([tri, tri], axis=1)
    const = lambda a: pl.BlockSpec(a.shape, lambda i, j: tuple(0 for _ in a.shape))
    consts = [sp["a_re"], sp["a_im"], sp["pneg_re"], sp["pneg_im"], sp["ppos_re"], sp["ppos_im"],
              sp["bb_re"], sp["bb_im"], sp["c_re"], sp["c_im"], d_skip.reshape(1, d), tri2]
    state = pl.BlockSpec((1, 1, gp), lambda i, j: (i, 0, 0))
    const_bytes = sum(a.size * a.dtype.itemsize for a in consts)
    z, h_re, h_im = pl.pallas_call(
        _ssm_seq_kernel,
        grid=(n, t // chunk),
        in_specs=[pl.BlockSpec((1, chunk, d), lambda i, j: (i, j, 0)),
                  pl.BlockSpec((1, 6, 1, d), lambda i, j: (i, 0, 0, 0)), state, state]
                 + [const(a) for a in consts],
        out_specs=[pl.BlockSpec((1, chunk, d), lambda i, j: (i, j, 0)), state, state],
        out_shape=[jax.ShapeDtypeStruct((n, t, d), BF16),
                   jax.ShapeDtypeStruct((n, 1, gp), F32), jax.ShapeDtypeStruct((n, 1, gp), F32)],
        name="ssm_seq",
        compiler_params=_cparams(("parallel", "arbitrary"), 2 * const_bytes + 40 * chunk * gp),
    )(x, mod, h0_re.reshape(n, 1, gp), h0_im.reshape(n, 1, gp), *consts)
    return z, h_re.reshape(n, gp), h_im.reshape(n, gp)


def _ssm_step_kernel(x_ref, mod_ref, h0r_ref, h0i_ref, ar_ref, ai_ref, bbr_ref, bbi_ref, cr_ref, ci_ref, d_ref,
                     z_ref, hr_ref, hi_ref):
    u = x_ref[0] * (1.0 + mod_ref[0, 1]) + mod_ref[0, 0]
    ub = u.astype(BF16)
    nb, kin, kst = bbr_ref.shape
    ys = []
    for n in range(nb):
        cs = slice(n * kst, (n + 1) * kst)
        ubn = ub[:, n * kin:(n + 1) * kin]
        c_re, c_im = _cmul(ar_ref[:, cs], ai_ref[:, cs], h0r_ref[:, cs], h0i_ref[:, cs])
        h_re = jnp.dot(ubn, bbr_ref[n], preferred_element_type=F32) + c_re
        h_im = jnp.dot(ubn, bbi_ref[n], preferred_element_type=F32) + c_im
        hr_ref[:, cs] = h_re
        hi_ref[:, cs] = h_im
        ys.append(jnp.dot(h_re.astype(BF16), cr_ref[n], preferred_element_type=F32)
                  + jnp.dot(h_im.astype(BF16), ci_ref[n], preferred_element_type=F32))
    y = jnp.concatenate(ys, axis=1) + u * d_ref[...]
    z_ref[0] = _gelu_tanh(y).astype(BF16)


def _ssm_step_call(x, mod, h0_re, h0_im, sp, d_skip):
    _, rows, d = x.shape
    gp = sp["a_re"].shape[1]
    args = [x, mod, h0_re, h0_im, sp["a_re"], sp["a_im"], sp["bb_re"], sp["bb_im"], sp["c_re"], sp["c_im"],
            d_skip.reshape(1, d)]
    full = lambda a: pl.BlockSpec(a.shape, lambda: tuple(0 for _ in a.shape))
    out_shapes = [jax.ShapeDtypeStruct((1, rows, d), BF16),
                  jax.ShapeDtypeStruct((rows, gp), F32), jax.ShapeDtypeStruct((rows, gp), F32)]
    return pl.pallas_call(
        _ssm_step_kernel,
        in_specs=[full(a) for a in args],
        out_specs=[full(s) for s in out_shapes],
        out_shape=out_shapes,
        name="ssm_step",
        compiler_params=_cparams(None, 4 * sum(a.size * a.dtype.itemsize for a in args)),
    )(*args)


def _mixer_post_kernel(a_ref, x_ref, mod_ref, w_ref, lng_ref, lnb_ref, *rest, glu, router, alpha):
    if router:
        wr_ref, br_ref, x1_ref, h2_ref, lg_ref = rest
    else:
        x1_ref, h2_ref = rest
    proj = jnp.dot(a_ref[0], w_ref[...], preferred_element_type=F32)
    if glu:
        d = proj.shape[1] // 2
        proj = proj[:, :d] * _sigmoid(proj[:, d:])
    x1 = _post_norm(x_ref[0], proj, mod_ref[0, 2], lng_ref[...], lnb_ref[...], alpha)
    x1_ref[0] = x1
    h2 = x1 * (1.0 + mod_ref[0, 4]) + mod_ref[0, 3]
    h2_ref[0] = h2.astype(BF16)
    if router:
        nt = (((1,), (1,)), ((), ()))
        h_hi, h_lo = _split_bf16(h2, 2)
        w_hi, w_lo = _split_bf16(wr_ref[...], 2)
        lg = (lax.dot_general(w_hi, h_hi, nt, preferred_element_type=F32)
              + lax.dot_general(w_lo, h_hi, nt, preferred_element_type=F32)
              + lax.dot_general(w_hi, h_lo, nt, preferred_element_type=F32))
        lg_ref[...] = lg + br_ref[...]


def _mixer_post_call(a, x, mod, w, ln_g, ln_b, alpha, *, glu, w_router=None, b_router=None):
    n, t, d = x.shape
    tm = _row_tile(t, ROW_TILE)
    rmod = mod.shape[2]
    router = w_router is not None
    row = lambda width: pl.BlockSpec((1, tm, width), lambda i, j: (i, j, 0))
    const = lambda arr: pl.BlockSpec(arr.shape, lambda i, j: tuple(0 for _ in arr.shape))
    mod_spec = pl.BlockSpec((1, 6, rmod, d), (lambda i, j: (i, 0, 0, 0)) if rmod == 1 else (lambda i, j: (i, 0, j, 0)))
    args = [a, x, mod, w, ln_g.reshape(1, d), ln_b.reshape(1, d)]
    in_specs = [row(a.shape[2]), row(d), mod_spec, const(w), const(args[4]), const(args[5])]
    out_specs = [row(d), row(d)]
    out_shape = [jax.ShapeDtypeStruct((n, t, d), F32), jax.ShapeDtypeStruct((n, t, d), BF16)]
    if router:
        e = w_router.shape[1]
        args += [w_router.T, b_router.reshape(e, 1)]
        in_specs += [const(args[-2]), const(args[-1])]
        out_specs.append(pl.BlockSpec((e, tm), lambda i, j: (0, i * (t // tm) + j)))
        out_shape.append(jax.ShapeDtypeStruct((e, n * t), F32))
    vmem = 2 * w.size * 2 + 12 * tm * w.shape[1] * 4 + 8 * tm * d * 4
    return pl.pallas_call(
        functools.partial(_mixer_post_kernel, glu=glu, router=router, alpha=alpha),
        grid=(n, t // tm), in_specs=in_specs, out_specs=out_specs, out_shape=out_shape,
        name="mixer_post",
        compiler_params=_cparams(("parallel", "parallel"), vmem),
    )(*args)


def _swiglu_chunks(h, wg_ref, wu_ref, wd_ref, lead, fc):
    f = wg_ref.shape[-1]
    acc = None
    for lo in range(0, f, fc):
        hi = min(lo + fc, f)
        g = jnp.dot(h, wg_ref[lead + (slice(None), slice(lo, hi))], preferred_element_type=F32)
        u = jnp.dot(h, wu_ref[lead + (slice(None), slice(lo, hi))], preferred_element_type=F32)
        a = (g * _sigmoid(g) * u).astype(BF16)
        part = jnp.dot(a, wd_ref[lead + (slice(lo, hi), slice(None))], preferred_element_type=F32)
        acc = part if acc is None else acc + part
    return acc


def _ffn_post_kernel(h_ref, x_ref, mod_ref, wg_ref, wu_ref, wd_ref, lng_ref, lnb_ref, o_ref, *, alpha, fc):
    out = _swiglu_chunks(h_ref[0], wg_ref, wu_ref, wd_ref, (), fc)
    o_ref[0] = _post_norm(x_ref[0], out, mod_ref[0, 5], lng_ref[...], lnb_ref[...], alpha)


def _ffn_post_call(h, x, mod, wg, wu, wd, ln_g, ln_b, alpha):
    n, t, d = x.shape
    f = wg.shape[1]
    tm = _row_tile(t, ROW_TILE)
    rmod = mod.shape[2]
    row = lambda: pl.BlockSpec((1, tm, d), lambda i, j: (i, j, 0))
    const = lambda arr: pl.BlockSpec(arr.shape, lambda i, j: tuple(0 for _ in arr.shape),
                                     pipeline_mode=pl.Buffered(1))
    mod_spec = pl.BlockSpec((1, 6, rmod, d), (lambda i, j: (i, 0, 0, 0)) if rmod == 1 else (lambda i, j: (i, 0, j, 0)))
    fc = 2 * V7X_MXU_DIM
    args = [h, x, mod, wg, wu, wd, ln_g.reshape(1, d), ln_b.reshape(1, d)]
    vmem = 3 * d * f * 2 + 10 * tm * d * 4 + 6 * tm * fc * 4
    return pl.pallas_call(
        functools.partial(_ffn_post_kernel, alpha=alpha, fc=fc),
        grid=(n, t // tm),
        in_specs=[row(), row(), mod_spec] + [const(a) for a in args[3:]],
        out_specs=row(),
        out_shape=jax.ShapeDtypeStruct((n, t, d), F32), name="ffn_post",
        compiler_params=_cparams(("parallel", "parallel"), vmem),
    )(*args)


MOE_DISPATCH_TILE = 512
MOE_ROW_TILE = 512
MOE_COMBINE_TILE = 128
BF16_ROWS = 16


def _route_kernel(lg_ref, upper_ref, gate_ref, sel_ref, rank_ref, cnt_ref):
    @pl.when(pl.program_id(0) == 0)
    def _():
        cnt_ref[...] = jnp.zeros_like(cnt_ref)

    lg = lg_ref[...]
    e = lg.shape[0]
    idx = lax.broadcasted_iota(jnp.int32, lg.shape, 0)
    m1 = jnp.max(lg, axis=0, keepdims=True)
    i1 = jnp.min(jnp.where(lg == m1, idx, e), axis=0, keepdims=True)
    rest = jnp.where(idx == i1, -jnp.inf, lg)
    m2 = jnp.max(rest, axis=0, keepdims=True)
    i2 = jnp.min(jnp.where(rest == m2, idx, e), axis=0, keepdims=True)
    e2 = jnp.exp(m2 - m1)
    den = 1.0 + e2
    gate_ref[...] = jnp.where(idx == i1, 1.0 / den, 0.0) + jnp.where(idx == i2, e2 / den, 0.0)
    sel = jnp.where(idx == i1, 1.0, 0.0) + jnp.where(idx == i2, 1.0, 0.0)
    sel_ref[...] = sel
    rank_ref[...] = cnt_ref[...] + jnp.dot(sel.astype(BF16), upper_ref[...], preferred_element_type=F32)
    cnt_ref[...] += jnp.sum(sel, axis=1, keepdims=True)


def _route_call(logits_t):
    e, m = logits_t.shape
    tr = _row_tile(m, ROW_TILE)
    upper = jnp.triu(jnp.ones((tr, tr), F32), k=1).astype(BF16)
    blk = pl.BlockSpec((e, tr), lambda i: (0, i))
    out_shape = [jax.ShapeDtypeStruct((e, m), F32)] * 3 + [jax.ShapeDtypeStruct((e, 1), F32)]
    return pl.pallas_call(
        _route_kernel, grid=(m // tr,),
        in_specs=[blk, pl.BlockSpec((tr, tr), lambda i: (0, 0))],
        out_specs=[blk, blk, blk, pl.BlockSpec((e, 1), lambda i: (0, 0))],
        out_shape=out_shape, name="moe_route",
        compiler_params=_cparams(("arbitrary",), 4 * tr * tr * 2 + 64 * e * tr * 4),
    )(logits_t, upper)


def _moe_schedule(gate_t, sel_t, rank_t, cnt, tt, tme, tc):
    e, m = sel_t.shape
    cnt_i = cnt[:, 0].astype(jnp.int32)
    seg_tiles = cnt_i // tme + 1
    seg_end = jnp.cumsum(seg_tiles)
    seg_start = seg_end - seg_tiles
    offsets = seg_start * tme
    n_tiles = (TOP_K * m) // tme + e
    tile_ids = jnp.arange(n_tiles, dtype=jnp.int32)
    tile_expert = jnp.minimum(jnp.sum(tile_ids[:, None] >= seg_end[None, :], axis=1), e - 1).astype(jnp.int32)
    tile_valid = (tile_ids < seg_end[-1]).astype(jnp.int32)
    rank_i = rank_t.astype(jnp.int32)
    rank_at = jnp.concatenate([rank_i[:, ::tt], cnt_i[:, None]], axis=1)
    done_tiles = rank_at // tme
    w0 = (rank_at - done_tiles * tme) // BF16_ROWS * BF16_ROWS
    completes = jnp.concatenate([rank_at[:, 1:] >= (done_tiles[:, :-1] + 1) * tme,
                                 jnp.ones((e, 1), bool)], axis=1).astype(jnp.int32)
    pos = jnp.where(sel_t > 0, offsets[:, None].astype(F32) + rank_t, -1.0).T
    blk0 = ((offsets[:, None] + rank_i[:, ::tc]) // tc).T
    return dict(base=(done_tiles * tme + w0).reshape(-1), w0=w0.reshape(-1), completes=completes.reshape(-1),
                out_tile=(seg_start[:, None] + done_tiles).reshape(-1), tile_expert=tile_expert,
                tile_valid=tile_valid, n_tiles=n_tiles, blk0=blk0.reshape(-1).astype(jnp.int32), pos=pos,
                gates=gate_t.T)


def _dispatch_kernel(base_ref, w0_ref, done_ref, tile_ref, h_ref, sel_ref, rank_ref, o_ref, stage_ref, *, tme):
    del tile_ref
    ex = pl.program_id(0)
    j = pl.program_id(1)
    steps = pl.num_programs(1)
    flat = ex * steps + j
    tt = h_ref.shape[0]
    win = tt + BF16_ROWS

    @pl.when(j == 0)
    def _():
        stage_ref[...] = jnp.zeros_like(stage_ref)

    @pl.when(j < steps - 1)
    def _():
        w0 = pl.multiple_of(w0_ref[flat], BF16_ROWS)
        local = rank_ref[pl.ds(ex, 1), :] - base_ref[flat].astype(F32)
        local = jnp.where(sel_ref[pl.ds(ex, 1), :] > 0, local, -1.0)
        rows = lax.broadcasted_iota(jnp.int32, (win, tt), 0).astype(F32)
        onehot = jnp.where(rows == local, 1.0, 0.0).astype(BF16)
        stage_ref[pl.ds(w0, win), :] += jnp.dot(onehot, h_ref[...], preferred_element_type=F32)

    @pl.when(done_ref[flat] == 1)
    def _():
        o_ref[...] = stage_ref[0:tme, :].astype(BF16)
        tail = stage_ref[tme:, :]
        stage_ref[...] = jnp.zeros_like(stage_ref)
        stage_ref[0:win, :] = tail


def _dispatch_call(h, sel_t, rank_t, sched, tt, tme):
    m, d = h.shape
    e = sel_t.shape[0]
    nj = m // tt
    steps = nj + 1
    clamp = lambda j: jnp.minimum(j, nj - 1)
    grid_spec = pltpu.PrefetchScalarGridSpec(
        num_scalar_prefetch=4, grid=(e, steps),
        in_specs=[pl.BlockSpec((tt, d), lambda ex, j, *_: (clamp(j), 0)),
                  pl.BlockSpec((e, tt), lambda ex, j, *_: (0, clamp(j))),
                  pl.BlockSpec((e, tt), lambda ex, j, *_: (0, clamp(j)))],
        out_specs=pl.BlockSpec((tme, d), lambda ex, j, base, w0, done, tile: (tile[ex * steps + j], 0)),
        scratch_shapes=[pltpu.VMEM((tme + tt + BF16_ROWS, d), F32)])
    return pl.pallas_call(
        functools.partial(_dispatch_kernel, tme=tme), grid_spec=grid_spec,
        out_shape=jax.ShapeDtypeStruct((sched["n_tiles"] * tme, d), BF16), name="moe_dispatch",
        compiler_params=_cparams(("arbitrary", "arbitrary"), 12 * (tme + tt) * d * 4),
    )(sched["base"], sched["w0"], sched["completes"], sched["out_tile"], h, sel_t, rank_t)


def _group_ffn_kernel(texp_ref, valid_ref, x_ref, wg_ref, wu_ref, wd_ref, y_ref, *, fc):
    del texp_ref
    live = valid_ref[pl.program_id(0)] == 1

    @pl.when(live)
    def _():
        y_ref[...] = _swiglu_chunks(x_ref[...], wg_ref, wu_ref, wd_ref, (0, 0), fc).astype(BF16)

    @pl.when(jnp.logical_not(live))
    def _():
        y_ref[...] = jnp.zeros_like(y_ref)


def _group_ffn_call(xs, sched, wg, wu, wd, layer, tme):
    rows, d = xs.shape
    f = wg.shape[3]
    fc = 2 * V7X_MXU_DIM
    row = pl.BlockSpec((tme, d), lambda i, texp, valid: (i, 0))
    grid_spec = pltpu.PrefetchScalarGridSpec(
        num_scalar_prefetch=2, grid=(rows // tme,),
        in_specs=[row,
                  pl.BlockSpec((1, 1, d, f), lambda i, texp, valid: (layer, texp[i], 0, 0)),
                  pl.BlockSpec((1, 1, d, f), lambda i, texp, valid: (layer, texp[i], 0, 0)),
                  pl.BlockSpec((1, 1, f, d), lambda i, texp, valid: (layer, texp[i], 0, 0))],
        out_specs=row)
    return pl.pallas_call(
        functools.partial(_group_ffn_kernel, fc=fc), grid_spec=grid_spec,
        out_shape=jax.ShapeDtypeStruct((rows, d), BF16), name="moe_group_ffn",
        compiler_params=_cparams(("arbitrary",), 2 * 3 * d * f * 2 + 8 * tme * d * 4 + 6 * tme * fc * 4),
    )(sched["tile_expert"], sched["tile_valid"], xs, wg, wu, wd)


def _combine_post_kernel(blk_ref, x_ref, mod_ref, pos_ref, gate_ref, lng_ref, lnb_ref, *rest, experts, alpha):
    y_refs, o_ref = rest[:2 * experts], rest[2 * experts]
    tc = x_ref.shape[1]
    tile = pl.program_id(0) * pl.num_programs(1) + pl.program_id(1)
    lane = lax.broadcasted_iota(jnp.int32, (tc, 2 * tc), 1).astype(F32)
    pos = pos_ref[...]
    gates = gate_ref[...]
    acc = jnp.zeros(x_ref.shape[1:], F32)
    for e in range(experts):
        base = (blk_ref[tile * experts + e] * tc).astype(F32)
        onehot = jnp.where(lane == pos[:, e:e + 1] - base, 1.0, 0.0).astype(BF16)
        window = jnp.concatenate([y_refs[2 * e][...], y_refs[2 * e + 1][...]], axis=0)
        acc = acc + gates[:, e:e + 1] * jnp.dot(onehot, window, preferred_element_type=F32)
    o_ref[0] = _post_norm(x_ref[0], acc, mod_ref[0, 5], lng_ref[...], lnb_ref[...], alpha)


def _combine_post_call(ys, x, mod, sched, ln_g, ln_b, alpha, tc):
    n, t, d = x.shape
    e = sched["pos"].shape[1]
    nj = t // tc
    nblk = ys.shape[0] // tc
    rmod = mod.shape[2]
    row = pl.BlockSpec((1, tc, d), lambda i, j, blk: (i, j, 0))
    per_tok = pl.BlockSpec((tc, e), lambda i, j, blk: (i * nj + j, 0))
    const = lambda arr: pl.BlockSpec(arr.shape, lambda i, j, blk: tuple(0 for _ in arr.shape))
    mod_spec = pl.BlockSpec((1, 6, rmod, d),
                            (lambda i, j, blk: (i, 0, 0, 0)) if rmod == 1 else (lambda i, j, blk: (i, 0, j, 0)))

    def gather_spec(ex, half):
        return pl.BlockSpec((tc, d), lambda i, j, blk: (jnp.minimum(blk[(i * nj + j) * e + ex] + half, nblk - 1), 0))

    lng, lnb = ln_g.reshape(1, d), ln_b.reshape(1, d)
    grid_spec = pltpu.PrefetchScalarGridSpec(
        num_scalar_prefetch=1, grid=(n, nj),
        in_specs=[row, mod_spec, per_tok, per_tok, const(lng), const(lnb)]
                 + [gather_spec(ex, half) for ex in range(e) for half in range(2)],
        out_specs=row)
    return pl.pallas_call(
        functools.partial(_combine_post_kernel, experts=e, alpha=alpha), grid_spec=grid_spec,
        out_shape=jax.ShapeDtypeStruct((n, t, d), F32), name="moe_combine_post",
        compiler_params=_cparams(("parallel", "parallel"), 4 * 2 * e * tc * d * 2 + 16 * tc * d * 4),
    )(sched["blk0"], x, mod, sched["pos"], sched["gates"], lng, lnb, *([ys] * (2 * e)))


def _moe_call(h2, logits_t, x1, mod, wg, wu, wd, layer, ln_g, ln_b, alpha):
    n, t, d = x1.shape
    m = n * t
    tt = min(MOE_DISPATCH_TILE, m)
    tme = min(MOE_ROW_TILE, m)
    tc = min(MOE_COMBINE_TILE, t)
    gate_t, sel_t, rank_t, cnt = _route_call(logits_t)
    sched = _moe_schedule(gate_t, sel_t, rank_t, cnt, tt, tme, tc)
    xs = _dispatch_call(h2.reshape(m, d), sel_t, rank_t, sched, tt, tme)
    ys = _group_ffn_call(xs, sched, wg, wu, wd, layer, tme)
    return _combine_post_call(ys, x1, mod, sched, ln_g, ln_b, alpha, tc)


def _log_sigmoid(z):
    return -(jnp.maximum(-z, 0.0) + jnp.log1p(jnp.exp(-jnp.abs(z))))


ATTN_LANES = V7X_LANES
AUG_TERMS = 3
LOG2E = 1.4426950408889634


def _aug_selectors(heads, hd, offset):
    rows = jnp.arange(AUG_TERMS * heads)
    col = (rows % heads) * ATTN_LANES + hd + offset + rows // heads
    return jnp.zeros((AUG_TERMS * heads, heads * ATTN_LANES), F32).at[rows, col].set(1.0).astype(BF16)


def _aug_ones(heads, hd, offset, count):
    lane = jnp.arange(heads * ATTN_LANES) % ATTN_LANES
    return ((lane >= hd + offset) & (lane < hd + offset + count)).astype(F32).reshape(1, heads * ATTN_LANES)


def _place_terms(x, sel_ref, sign):
    terms = jnp.concatenate([t.astype(F32) for t in _split_bf16(x, AUG_TERMS)], axis=1).astype(BF16)
    return sign * jnp.dot(terms, sel_ref[...], preferred_element_type=F32)


def _store_head_operands(dense, tail, out_ref, hd):
    per_tile = ATTN_LANES // hd
    own = lax.broadcasted_iota(jnp.int32, (dense.shape[0], ATTN_LANES), 1) < hd
    for h in range(out_ref.shape[1]):
        tile = dense[:, (h // per_tile) * ATTN_LANES:(h // per_tile + 1) * ATTN_LANES]
        if h % per_tile:
            tile = pltpu.roll(tile, ATTN_LANES - (h % per_tile) * hd, axis=1)
        out_ref[0, h] = jnp.where(own, tile, tail[:, h * ATTN_LANES:(h + 1) * ATTN_LANES]).astype(BF16)


def _kv_kernel(x_ref, wk_ref, wv_ref, wf_ref, bf_ref, *rest, cumulative):
    if cumulative:
        tri_ref, sel_ref, kone_ref, vone_ref, k_ref, v_ref, lf_ref, fk_ref, ka_ref, va_ref, carry_ref = rest
    else:
        k_ref, v_ref, lf_ref = rest
    xb = x_ref[0].astype(BF16)
    k = jnp.dot(xb, wk_ref[...], preferred_element_type=F32)
    v = jnp.dot(xb, wv_ref[...], preferred_element_type=F32)
    k_ref[0] = k
    v_ref[0] = v
    lf = _log_sigmoid(jnp.dot(xb, wf_ref[...], preferred_element_type=F32) + bf_ref[...])
    lf_ref[0] = lf
    if cumulative:
        @pl.when(pl.program_id(1) == 0)
        def _():
            carry_ref[...] = jnp.zeros_like(carry_ref)

        fk = carry_ref[...] + jnp.dot(tri_ref[...], jnp.concatenate(_split_bf16(lf, 3), axis=0),
                                      preferred_element_type=F32)
        fk_ref[0] = fk
        carry_ref[...] = fk[fk.shape[0] - 1:]
        hd = k.shape[1] // ka_ref.shape[1]
        _store_head_operands(k, kone_ref[...] + _place_terms(fk * LOG2E, sel_ref, -1.0), ka_ref, hd)
        _store_head_operands(v, jnp.broadcast_to(vone_ref[...], (v.shape[0], vone_ref.shape[1])), va_ref, hd)


def _kv_call(x, wk, wv, wf, bf, *, cumulative):
    n, t, d = x.shape
    hh = wf.shape[1]
    hd = d // hh
    tm = _row_tile(t, ROW_TILE)
    row = lambda width: pl.BlockSpec((1, tm, width), lambda i, j: (i, j, 0))
    const = lambda arr: pl.BlockSpec(arr.shape, lambda i, j: tuple(0 for _ in arr.shape))
    args = [x, wk, wv, wf, bf.reshape(1, hh)]
    out_specs = [row(d), row(d), row(hh)]
    out_shape = [jax.ShapeDtypeStruct((n, t, d), F32)] * 2 + [jax.ShapeDtypeStruct((n, t, hh), F32)]
    scratch = []
    if cumulative:
        tri = jnp.tril(jnp.ones((tm, tm), F32)).astype(BF16)
        args += [jnp.concatenate([tri] * 3, axis=1),
                 _aug_selectors(hh, hd, AUG_TERMS), _aug_ones(hh, hd, 0, AUG_TERMS), _aug_ones(hh, hd, 0, 1)]
        head_rows = pl.BlockSpec((1, hh, tm, ATTN_LANES), lambda i, j: (i, 0, j, 0))
        out_specs += [row(hh), head_rows, head_rows]
        out_shape += [jax.ShapeDtypeStruct((n, t, hh), F32)] + [jax.ShapeDtypeStruct((n, hh, t, ATTN_LANES), BF16)] * 2
        scratch = [pltpu.VMEM((1, hh), F32)]
    return pl.pallas_call(
        functools.partial(_kv_kernel, cumulative=cumulative),
        grid=(n, t // tm),
        in_specs=[row(d)] + [const(a) for a in args[1:]],
        out_specs=out_specs, out_shape=out_shape, scratch_shapes=scratch, name="kv_proj",
        compiler_params=_cparams(("parallel", "arbitrary"), 16 * d * d + 40 * tm * d * 4),
    )(*args)


def _q_kernel(x_ref, mod_ref, w_ref, *rest, scale, augmented):
    h = (x_ref[0] * (1.0 + mod_ref[0, 1]) + mod_ref[0, 0]).astype(BF16)
    q = jnp.dot(h, w_ref[...], preferred_element_type=F32) * scale
    if augmented:
        fq_ref, sel_ref, one_ref, q_ref = rest
        _store_head_operands(q, one_ref[...] + _place_terms(fq_ref[0] * LOG2E, sel_ref, 1.0), q_ref,
                             q.shape[1] // q_ref.shape[1])
    else:
        q_ref, = rest
        q_ref[0] = q.astype(BF16)


def _q_call(x, mod, wq, heads, fq=None):
    n, t, d = x.shape
    hd = d // heads
    tm = _row_tile(t, ROW_TILE)
    rmod = mod.shape[2]
    row = lambda width: pl.BlockSpec((1, tm, width), lambda i, j: (i, j, 0))
    const = lambda arr: pl.BlockSpec(arr.shape, lambda i, j: tuple(0 for _ in arr.shape))
    mod_spec = pl.BlockSpec((1, 6, rmod, d), (lambda i, j: (i, 0, 0, 0)) if rmod == 1 else (lambda i, j: (i, 0, j, 0)))
    augmented = fq is not None
    if augmented:
        args = [x, mod, wq, fq, _aug_selectors(heads, hd, 0),
                _aug_ones(heads, hd, AUG_TERMS, AUG_TERMS)]
        in_specs = [row(d), mod_spec, const(args[2]), row(heads), const(args[4]), const(args[5])]
        out_specs = pl.BlockSpec((1, heads, tm, ATTN_LANES), lambda i, j: (i, 0, j, 0))
        out_shape = jax.ShapeDtypeStruct((n, heads, t, ATTN_LANES), BF16)
        scale = hd ** -0.5 * LOG2E
    else:
        args = [x, mod, wq]
        in_specs = [row(d), mod_spec, const(wq)]
        out_specs = row(d)
        out_shape = jax.ShapeDtypeStruct((n, t, d), BF16)
        scale = hd ** -0.5
    return pl.pallas_call(
        functools.partial(_q_kernel, scale=scale, augmented=augmented),
        grid=(n, t // tm), in_specs=in_specs, out_specs=out_specs, out_shape=out_shape, name="q_proj",
        compiler_params=_cparams(("parallel", "parallel"), 8 * d * d + 24 * tm * d * 4),
    )(*args)


ATTN_ROW_BLOCK = 128
ATTN_HEAD_GROUP = 16


def _attn_kernel(q_ref, k_ref, v_ref, o_ref, m_ref, acc_ref, *, hd):
    qi = pl.program_id(1)
    kj = pl.program_id(2)
    heads, tq = q_ref.shape[1], q_ref.shape[2]
    tk = k_ref.shape[2]
    rb = min(ATTN_ROW_BLOCK, tq)
    group = math.gcd(heads, ATTN_HEAD_GROUP)
    nt = (((1,), (1,)), ((), ()))

    @pl.when(kj == 0)
    def _():
        m_ref[...] = jnp.full_like(m_ref, NEG_INF)
        acc_ref[...] = jnp.zeros_like(acc_ref)

    def sweep(diagonal):
        blocks = [slice(r * rb, (r + 1) * rb) for r in range(tq // rb)]

        def group_body(g, carry):
            hs = [g * group + i for i in range(group)]
            scores = [[lax.dot_general(q_ref[0, h, rows, :], k_ref[0, h], nt, preferred_element_type=F32)
                       for rows in blocks] for h in hs]
            for h, s_h in zip(hs, scores):
                vh = v_ref[0, h]
                m_all = m_ref[h]
                acc_all = acc_ref[h]
                m_out, acc_out = [], []
                for r, (rows, s) in enumerate(zip(blocks, s_h)):
                    if diagonal:
                        qpos = r * rb + lax.broadcasted_iota(jnp.int32, (rb, tk), 0)
                        kpos = lax.broadcasted_iota(jnp.int32, (rb, tk), 1)
                        s = jnp.where(kpos <= qpos, s, NEG_INF)
                    m_prev = m_all[rows]
                    m_new = jnp.maximum(m_prev, jnp.max(s, axis=-1, keepdims=True))
                    p = jnp.exp2(s - m_new)
                    acc_out.append(jnp.exp2(m_prev - m_new) * acc_all[rows]
                                   + jnp.dot(p.astype(BF16), vh, preferred_element_type=F32))
                    m_out.append(m_new)
                m_ref[h] = jnp.concatenate(m_out, axis=0)
                acc_ref[h] = jnp.concatenate(acc_out, axis=0)
            return carry
        lax.fori_loop(0, heads // group, group_body, 0)

    @pl.when(kj < qi)
    def _():
        sweep(False)

    @pl.when(kj == qi)
    def _():
        sweep(True)
        for h in range(heads):
            a = acc_ref[h]
            o_ref[0, :, h * hd:(h + 1) * hd] = (a[:, :hd] / a[:, hd:hd + 1]).astype(BF16)


def _attn_call(qa, ka, va, hd):
    n, heads, t, lanes = qa.shape
    tq = tk = _row_tile(t, ATTN_TILE)
    kv_spec = pl.BlockSpec((1, heads, tk, lanes), lambda i, a, b: (i, 0, jnp.minimum(a, b), 0))
    return pl.pallas_call(
        functools.partial(_attn_kernel, hd=hd),
        grid=(n, t // tq, t // tk),
        in_specs=[pl.BlockSpec((1, heads, tq, lanes), lambda i, a, b: (i, 0, a, 0)), kv_spec, kv_spec],
        out_specs=pl.BlockSpec((1, tq, heads * hd), lambda i, a, b: (i, a, 0)),
        out_shape=jax.ShapeDtypeStruct((n, t, heads * hd), BF16),
        scratch_shapes=[pltpu.VMEM((heads, tq, 1), F32), pltpu.VMEM((heads, tq, lanes), F32)],
        name="causal_attn",
        compiler_params=_cparams(("parallel", "parallel", "arbitrary"),
                                 2 * heads * tq * lanes * 4 + 8 * heads * tq * lanes * 2
                                 + 3 * math.gcd(heads, ATTN_HEAD_GROUP) * tq * tk * 4),
    )(qa, ka, va)


def _column_to_row(col, eye):
    return jnp.sum(jnp.where(eye > 0, col, 0.0), axis=0, keepdims=True)


DECODE_PAGES_PER_STEP = 8


def _decode_attn_kernel(pt_ref, qb_ref, *rest, g):
    del pt_ref
    kt_refs, vt_refs, lft_refs = rest[:g], rest[g:2 * g], rest[2 * g:3 * g]
    (knew_ref, vnew_ref, lfnew_ref, later_ref, diag_ref, eye_ref, o_ref, m_ref, l_ref, acc_ref, suf_ref) = rest[3 * g:]
    step = pl.program_id(1)
    qb = qb_ref[0]
    eye = eye_ref[...]

    @pl.when(step == 0)
    def _():
        m_ref[...] = jnp.dot(qb, knew_ref[0].astype(BF16), preferred_element_type=F32)
        l_ref[...] = jnp.ones_like(l_ref)
        acc_ref[...] = jnp.broadcast_to(vnew_ref[0], acc_ref.shape)
        suf_ref[...] = lfnew_ref[0]

    scores = []
    after = suf_ref[...]
    for kt_ref, lft_ref in zip(kt_refs, lft_refs):
        lf = lft_ref[0]
        lf3 = jnp.concatenate(_split_bf16(lf, 3), axis=1)
        scores.append(jnp.dot(qb, kt_ref[0].astype(BF16), preferred_element_type=F32)
                      + jnp.dot(lf3, later_ref[...], preferred_element_type=F32) + after)
        after = after + jnp.sum(lf, axis=1, keepdims=True)
    suf_ref[...] = after
    m_prev = m_ref[...]
    m_new = m_prev
    for s in scores:
        m_new = jnp.maximum(m_new, jnp.max(s, axis=1, keepdims=True))
    alpha = jnp.exp(m_prev - m_new)
    l_new = alpha * l_ref[...]
    acc = _column_to_row(alpha, eye) * acc_ref[...]
    for s, vt_ref in zip(scores, vt_refs):
        p = jnp.exp(s - m_new)
        l_new = l_new + jnp.sum(p, axis=1, keepdims=True)
        acc = acc + lax.dot_general(vt_ref[0].astype(BF16), p.astype(BF16), (((1,), (1,)), ((), ())),
                                    preferred_element_type=F32)
    l_ref[...] = l_new
    acc_ref[...] = acc
    m_ref[...] = m_new

    @pl.when(step == pl.num_programs(1) - 1)
    def _():
        on_diag = diag_ref[...] > 0
        num = jnp.sum(jnp.where(on_diag, acc_ref[...], 0.0), axis=1, keepdims=True)
        den = jnp.sum(jnp.where(on_diag, _column_to_row(l_ref[...], eye), 0.0), axis=1, keepdims=True)
        o_ref[0] = (num / den).astype(BF16)


def _decode_attn_call(page_table, q, cache_k, cache_v, cache_logf, k_new, v_new, lf_new, heads):
    nseq, npages = page_table.shape
    n_phys, page, _, hd = cache_k.shape
    d = heads * hd
    seg = jnp.repeat(jnp.eye(heads, dtype=F32), hd, axis=1)
    qb = q[:, None, :] * seg[None].astype(BF16)
    later = jnp.tril(jnp.ones((page, page), F32), k=-1).astype(BF16)
    later3 = jnp.concatenate([later] * 3, axis=0)
    kt = cache_k.transpose(0, 2, 3, 1).reshape(n_phys, d, page)
    vt = cache_v.transpose(0, 2, 3, 1).reshape(n_phys, d, page)
    lft = cache_logf.transpose(0, 2, 1)
    g = math.gcd(npages, DECODE_PAGES_PER_STEP)

    def paged(rows):
        return [pl.BlockSpec((1, rows, page), functools.partial(
            lambda b, p, pt, i: (pt[b, npages - 1 - (p * g + i)], 0, 0), i=i)) for i in range(g)]

    per_seq = lambda arr: pl.BlockSpec((1,) + arr.shape[1:], lambda b, p, pt: (b,) + tuple(0 for _ in arr.shape[1:]))
    const = lambda arr: pl.BlockSpec(arr.shape, lambda b, p, pt: tuple(0 for _ in arr.shape))
    tail = [k_new.reshape(nseq, d, 1), v_new.reshape(nseq, d, 1), lf_new.reshape(nseq, heads, 1),
            later3, seg.T, jnp.eye(heads, dtype=F32)]
    args = [qb] + [kt] * g + [vt] * g + [lft] * g + tail
    grid_spec = pltpu.PrefetchScalarGridSpec(
        num_scalar_prefetch=1, grid=(nseq, npages // g),
        in_specs=[per_seq(qb)] + paged(d) + paged(d) + paged(heads)
                 + [per_seq(a) for a in tail[:3]] + [const(a) for a in tail[3:]],
        out_specs=pl.BlockSpec((1, d, 1), lambda b, p, pt: (b, 0, 0)),
        scratch_shapes=[pltpu.VMEM((heads, 1), F32), pltpu.VMEM((heads, 1), F32), pltpu.VMEM((d, heads), F32),
                        pltpu.VMEM((heads, 1), F32)])
    out = pl.pallas_call(
        functools.partial(_decode_attn_kernel, g=g), grid_spec=grid_spec,
        out_shape=jax.ShapeDtypeStruct((nseq, d, 1), BF16), name="paged_decode_attn",
        compiler_params=_cparams(("parallel", "arbitrary"), (8 + 7 * g) * page * d * 4),
    )(page_table, *args)
    return out.reshape(1, nseq, d)


def _trunk(x, mods, h0_re, h0_im, wts, ssm, *, sequence, paged=None):
    depth = mods.shape[0]
    n_a = wts["w_glu"].shape[0]
    heads = wts["w_f"].shape[1]
    nb, rows, d = x.shape
    hd = d // heads
    alpha = (2.0 * depth) ** 0.25
    new_re, new_im = [], []
    k_new = v_new = lf_new = None
    fk = k_aug = v_aug = None
    for l in range(depth):
        mod = mods[l]
        moe = l % 2 == 1
        li = l // 2
        router = dict(w_router=wts["w_router"][li], b_router=wts["b_router"][li]) if moe else {}
        if l < n_a:
            if sequence:
                z, hr, hi = _ssm_seq_call(x, mod, h0_re[l], h0_im[l], ssm[l], wts["ssm_d"][l])
            else:
                z, hr, hi = _ssm_step_call(x, mod, h0_re[l], h0_im[l], ssm[l], wts["ssm_d"][l])
            new_re.append(hr)
            new_im.append(hi)
            res = _mixer_post_call(z, x, mod, wts["w_glu"][l], wts["ln_g"][l, 0], wts["ln_b"][l, 0], alpha,
                                   glu=True, **router)
        else:
            lb = l - n_a
            if sequence:
                o = _attn_call(_q_call(x, mod, wts["w_q"][lb], heads, fq=fk), k_aug, v_aug, hd)
            else:
                q = _q_call(x, mod, wts["w_q"][lb], heads)
                o = _decode_attn_call(paged[0], q[0], paged[1], paged[2], paged[3], k_new[0], v_new[0], lf_new[0],
                                      heads)
            res = _mixer_post_call(o, x, mod, wts["w_o"][lb], wts["ln_g"][l, 0], wts["ln_b"][l, 0], alpha,
                                   glu=False, **router)
        if moe:
            x1, h2, logits_t = res
            x = _moe_call(h2, logits_t, x1, mod, wts["w_exp_gate"], wts["w_exp_up"], wts["w_exp_down"], li,
                          wts["ln_g"][l, 1], wts["ln_b"][l, 1], alpha)
        else:
            x1, h2 = res
            x = _ffn_post_call(h2, x1, mod, wts["w_ff_gate"][li], wts["w_ff_up"][li], wts["w_ff_down"][li],
                               wts["ln_g"][l, 1], wts["ln_b"][l, 1], alpha)
        if l == n_a - 1:
            outs = _kv_call(x, wts["w_k"], wts["w_v"], wts["w_f"], wts["b_f"], cumulative=sequence)
            k_new, v_new, lf_new = outs[:3]
            if sequence:
                fk, k_aug, v_aug = outs[3:]
    return x, jnp.stack(new_re), jnp.stack(new_im), k_new, v_new, lf_new


def kernel(x_prompt, x_sample, c_prompt, c_sample, cache_k, cache_v, cache_logf, state_ssm_re, state_ssm_im,
           page_table, w_mod, b_mod, ln_g, ln_b, ssm_lam_re, ssm_lam_im, ssm_log_dt, ssm_b_re, ssm_b_im,
           ssm_c_re, ssm_c_im, ssm_d, w_glu, w_k, w_v, w_f, b_f, w_q, w_o, w_ff_gate, w_ff_up, w_ff_down,
           w_router, b_router, w_exp_gate, w_exp_up, w_exp_down):
    batch, seq, d = x_prompt.shape
    dec_batch, dec_seq, _ = x_sample.shape
    assert dec_seq == 1
    depth = w_mod.shape[0]
    n_a, g, p = ssm_lam_re.shape
    heads = w_f.shape[1]
    hd = d // heads
    gp = g * p

    n_c = batch + dec_batch
    c_all = jnp.concatenate([c_prompt, c_sample], axis=0)
    c_all = jnp.pad(c_all, ((0, (-n_c) % 8), (0, 0)))
    mods = _mod_call(c_all, w_mod, b_mod)
    mods_p = mods[:, :batch].reshape(depth, batch, 6, 1, d)
    mods_s = mods[:, batch:n_c].reshape(depth, dec_batch, 6, d).transpose(0, 2, 1, 3)[:, None]

    bf = lambda w: w.astype(BF16)
    wts = dict(ln_g=ln_g, ln_b=ln_b, ssm_d=ssm_d, w_glu=bf(w_glu), w_k=bf(w_k), w_v=bf(w_v), w_f=bf(w_f), b_f=b_f,
               w_q=bf(w_q), w_o=bf(w_o), w_ff_gate=bf(w_ff_gate), w_ff_up=bf(w_ff_up), w_ff_down=bf(w_ff_down),
               w_router=w_router, b_router=b_router, w_exp_gate=bf(w_exp_gate), w_exp_up=bf(w_exp_up),
               w_exp_down=bf(w_exp_down))
    ssm = [_ssm_prep(ssm_lam_re[l], ssm_lam_im[l], ssm_log_dt[l], ssm_b_re[l], ssm_b_im[l], ssm_c_re[l],
                     ssm_c_im[l]) for l in range(n_a)]

    h0 = jnp.zeros((n_a, batch, gp), F32)
    y_p, re_p, im_p, k_p, v_p, lf_p = _trunk(x_prompt, mods_p, h0, h0, wts, ssm, sequence=True)

    x_s = x_sample.reshape(1, dec_batch, d)
    y_s, re_s, im_s, k_s, v_s, lf_s = _trunk(
        x_s, mods_s, state_ssm_re.reshape(n_a, dec_batch, gp), state_ssm_im.reshape(n_a, dec_batch, gp), wts, ssm,
        sequence=False, paged=(page_table, cache_k, cache_v, cache_logf))

    return (y_p, y_s.reshape(dec_batch, 1, d),
            re_p.reshape(n_a, batch, g, p), im_p.reshape(n_a, batch, g, p),
            k_p.reshape(batch, seq, heads, hd), v_p.reshape(batch, seq, heads, hd), lf_p,
            re_s.reshape(n_a, dec_batch, g, p), im_s.reshape(n_a, dec_batch, g, p),
            k_s.reshape(dec_batch, 1, heads, hd), v_s.reshape(dec_batch, 1, heads, hd),
            lf_s.reshape(dec_batch, 1, heads))
```

```python
import functools
import math

import jax
import jax.numpy as jnp
from jax import lax
from jax.experimental import pallas as pl
from jax.experimental.pallas import tpu as pltpu

F32 = jnp.float32
BF16 = jnp.bfloat16

GROUP_SIZE = 16
TOP_K = 2
LN_EPS = 1e-5
NEG_INF = -1e30

V7X_MXU_DIM = 256
V7X_LANES = 128
V7X_VMEM_BUDGET = 56 * 1024 * 1024

SSM_CHUNK = 128
ROW_TILE = 512
ATTN_TILE = 512
ATTN_QUERY_TILE = 512


def _cparams(semantics, vmem_bytes):
    return pltpu.CompilerParams(dimension_semantics=semantics,
                                vmem_limit_bytes=int(min(max(vmem_bytes, 16 * 1024 * 1024), V7X_VMEM_BUDGET)))


def _row_tile(t, pref):
    tile = min(t, pref)
    assert t % tile == 0, (t, tile)
    return tile


def _sigmoid(x):
    return 1.0 / (1.0 + jnp.exp(-x))


def _split_bf16(x, parts):
    out = []
    r = x
    for _ in range(parts):
        p = r.astype(BF16)
        out.append(p)
        r = r - p.astype(F32)
    return out


def _post_norm(x, out, gate, g, b, alpha):
    y = alpha * x + (1.0 + gate) * out
    mu = jnp.mean(y, axis=-1, keepdims=True)
    d = y - mu
    var = jnp.mean(d * d, axis=-1, keepdims=True)
    return d * lax.rsqrt(var + LN_EPS) * g + b


def _mod_kernel(c_ref, w_ref, b_ref, o_ref):
    c = c_ref[...]
    s = (c * _sigmoid(c)).astype(BF16)
    o_ref[0] = jnp.dot(s, w_ref[0].astype(BF16), preferred_element_type=F32) + b_ref[0]


def _mod_call(c_all, w_mod, b_mod):
    depth, d, d6 = w_mod.shape
    r = c_all.shape[0]
    tn = _row_tile(d6, 1536)
    return pl.pallas_call(
        _mod_kernel,
        grid=(depth, d6 // tn),
        in_specs=[pl.BlockSpec((r, d), lambda l, j: (0, 0)),
                  pl.BlockSpec((1, d, tn), lambda l, j: (l, 0, j)),
                  pl.BlockSpec((1, 1, tn), lambda l, j: (l, 0, j))],
        out_specs=pl.BlockSpec((1, r, tn), lambda l, j: (l, 0, j)),
        out_shape=jax.ShapeDtypeStruct((depth, r, d6), F32),
        name="adaln_mod",
        compiler_params=_cparams(("parallel", "parallel"), 3 * d * tn * 4),
    )(c_all, w_mod, b_mod.reshape(depth, 1, d6))


def _discretise(lr, li, ldt):
    dt = jnp.exp(ldt)
    mag = jnp.exp(lr * dt)
    a_re = mag * jnp.cos(li * dt)
    a_im = mag * jnp.sin(li * dt)
    den = lr * lr + li * li
    nr = a_re - 1.0
    k_re = (nr * lr + a_im * li) / den
    k_im = (a_im * lr - nr * li) / den
    return a_re, a_im, k_re, k_im


def _cmul(ar, ai, br, bi):
    return ar * br - ai * bi, ar * bi + ai * br


def _ssm_prep_kernel(lr_row, li_row, dt_row, lr_col, li_col, dt_col, b_re, b_im,
                     a_re_o, a_im_o, pneg_re_o, pneg_im_o, ppos_re_o, ppos_im_o, bb_re_o, bb_im_o):
    a_re, a_im, _, _ = _discretise(lr_row[...], li_row[...], dt_row[...])
    a_re_o[...] = a_re
    a_im_o[...] = a_im
    _, _, k_re, k_im = _discretise(lr_col[...], li_col[...], dt_col[...])
    br = b_re[...]
    bi = b_im[...]
    bb_re_o[...] = k_re * br - k_im * bi
    bb_im_o[...] = k_re * bi + k_im * br

    chunk = ppos_re_o.shape[0]
    t = lax.broadcasted_iota(jnp.int32, ppos_re_o.shape, 0)
    n2 = a_re * a_re + a_im * a_im
    for (sq_re, sq_im, o_re, o_im) in ((a_re, a_im, ppos_re_o, ppos_im_o),
                                       (a_re / n2, -a_im / n2, pneg_re_o, pneg_im_o)):
        p_re = jnp.ones(ppos_re_o.shape, F32)
        p_im = jnp.zeros(ppos_re_o.shape, F32)
        bit = 1
        while bit < chunk:
            on = (t & bit) != 0
            f_re = jnp.where(on, sq_re, 1.0)
            f_im = jnp.where(on, sq_im, 0.0)
            p_re, p_im = _cmul(p_re, p_im, f_re, f_im)
            sq_re, sq_im = _cmul(sq_re, sq_im, sq_re, sq_im)
            bit *= 2
        o_re[...] = p_re
        o_im[...] = p_im


def _ssm_prep(lam_re, lam_im, log_dt, b_re, b_im, c_re, c_im):
    g, p = lam_re.shape
    gs = b_re.shape[-1]
    gp = g * p
    dt = jnp.broadcast_to(log_dt[:, None], (g, p))
    rows = [a.reshape(1, gp) for a in (lam_re, lam_im, dt)]
    cols = [a.reshape(gp, 1) for a in (lam_re, lam_im, dt)]
    full = lambda shape: pl.BlockSpec(shape, lambda: tuple(0 for _ in shape))
    out_shapes = ([jax.ShapeDtypeStruct((1, gp), F32)] * 2 + [jax.ShapeDtypeStruct((SSM_CHUNK, gp), F32)] * 4
                  + [jax.ShapeDtypeStruct((gp, gs), F32)] * 2)
    a_re, a_im, pneg_re, pneg_im, ppos_re, ppos_im, bb_re, bb_im = pl.pallas_call(
        _ssm_prep_kernel,
        in_specs=[full((1, gp))] * 3 + [full((gp, 1))] * 3 + [full((gp, gs))] * 2,
        out_specs=[full(s.shape) for s in out_shapes],
        out_shape=out_shapes,
        name="ssm_prep",
        compiler_params=_cparams(None, 48 * SSM_CHUNK * gp * 4),
    )(*rows, *cols, b_re.reshape(gp, gs), b_im.reshape(gp, gs))

    gpb = V7X_MXU_DIM // gs
    nb = g // gpb
    eye = jnp.eye(gpb, dtype=F32)

    def in_proj(bb):
        t = bb.reshape(nb, gpb, p, gs).transpose(0, 1, 3, 2)
        return jnp.einsum("nghp,gk->nghkp", t, eye).reshape(nb, gpb * gs, gpb * p).astype(BF16)

    def out_proj(c):
        t = c.reshape(nb, gpb, gs, p)
        return jnp.einsum("nghp,gk->nkpgh", t, eye).reshape(nb, gpb * p, gpb * gs).astype(BF16)

    return dict(a_re=a_re, a_im=a_im, pneg_re=pneg_re, pneg_im=pneg_im, ppos_re=ppos_re, ppos_im=ppos_im,
                bb_re=in_proj(bb_re), bb_im=in_proj(bb_im), c_re=out_proj(c_re), c_im=out_proj(-c_im))


def _gelu_tanh(y):
    return 0.5 * y * (1.0 + jnp.tanh(0.7978845608028654 * (y + 0.044715 * (y * y * y))))


def _ssm_seq_kernel(x_ref, mod_ref, h0r_ref, h0i_ref, ar_ref, ai_ref, pnr_ref, pni_ref, ppr_ref, ppi_ref,
                    bbr_ref, bbi_ref, cr_ref, ci_ref, d_ref, tri_ref, z_ref, hr_ref, hi_ref):
    step = pl.program_id(1)

    @pl.when(step == 0)
    def _():
        hr_ref[0] = h0r_ref[0]
        hi_ref[0] = h0i_ref[0]

    chunk = x_ref.shape[1]
    u = x_ref[0] * (1.0 + mod_ref[0, 1]) + mod_ref[0, 0]
    ub = u.astype(BF16)
    nb, kin, kst = bbr_ref.shape
    row0 = lax.broadcasted_iota(jnp.int32, (chunk, kst), 0) == 0
    tri = tri_ref[...]
    cols = [slice(n * kst, (n + 1) * kst) for n in range(nb)]
    carry = [_cmul(ar_ref[:, cs], ai_ref[:, cs], hr_ref[0, :, cs], hi_ref[0, :, cs]) for cs in cols]
    xs = [(jnp.dot(ub[:, n * kin:(n + 1) * kin], bbr_ref[n], preferred_element_type=F32),
           jnp.dot(ub[:, n * kin:(n + 1) * kin], bbi_ref[n], preferred_element_type=F32)) for n in range(nb)]
    cums = []
    for cs, (x_re, x_im), (c_re, c_im) in zip(cols, xs, carry):
        x_re = x_re + jnp.where(row0, c_re, 0.0)
        x_im = x_im + jnp.where(row0, c_im, 0.0)
        scaled = _cmul(x_re, x_im, pnr_ref[:, cs], pni_ref[:, cs])
        cums.append([jnp.dot(tri, jnp.concatenate(_split_bf16(s, 2), axis=0), preferred_element_type=F32)
                     for s in scaled])
    ys = []
    for n, (cs, cum) in enumerate(zip(cols, cums)):
        h_re, h_im = _cmul(cum[0], cum[1], ppr_ref[:, cs], ppi_ref[:, cs])
        hr_ref[0, :, cs] = h_re[chunk - 1:chunk]
        hi_ref[0, :, cs] = h_im[chunk - 1:chunk]
        ys.append(jnp.dot(h_re.astype(BF16), cr_ref[n], preferred_element_type=F32)
                  + jnp.dot(h_im.astype(BF16), ci_ref[n], preferred_element_type=F32))
    y = jnp.concatenate(ys, axis=1) + u * d_ref[...]
    z_ref[0] = _gelu_tanh(y).astype(BF16)


def _ssm_seq_call(x, mod, h0_re, h0_im, sp, d_skip):
    n, t, d = x.shape
    gp = sp["a_re"].shape[1]
    chunk = SSM_CHUNK
    assert t % chunk == 0
    tri = jnp.tril(jnp.ones((chunk, chunk), F32)).astype(BF16)
    tri2 = jnp.concatenate([tri, tri], axis=1)
    const = lambda a: pl.BlockSpec(a.shape, lambda i, j: tuple(0 for _ in a.shape))
    consts = [sp["a_re"], sp["a_im"], sp["pneg_re"], sp["pneg_im"], sp["ppos_re"], sp["ppos_im"],
              sp["bb_re"], sp["bb_im"], sp["c_re"], sp["c_im"], d_skip.reshape(1, d), tri2]
    state = pl.BlockSpec((1, 1, gp), lambda i, j: (i, 0, 0))
    const_bytes = sum(a.size * a.dtype.itemsize for a in consts)
    z, h_re, h_im = pl.pallas_call(
        _ssm_seq_kernel,
        grid=(n, t // chunk),
        in_specs=[pl.BlockSpec((1, chunk, d), lambda i, j: (i, j, 0)),
                  pl.BlockSpec((1, 6, 1, d), lambda i, j: (i, 0, 0, 0)), state, state]
                 + [const(a) for a in consts],
        out_specs=[pl.BlockSpec((1, chunk, d), lambda i, j: (i, j, 0)), state, state],
        out_shape=[jax.ShapeDtypeStruct((n, t, d), BF16),
                   jax.ShapeDtypeStruct((n, 1, gp), F32), jax.ShapeDtypeStruct((n, 1, gp), F32)],
        name="ssm_seq",
        compiler_params=_cparams(("parallel", "arbitrary"), 2 * const_bytes + 40 * chunk * gp),
    )(x, mod, h0_re.reshape(n, 1, gp), h0_im.reshape(n, 1, gp), *consts)
    return z, h_re.reshape(n, gp), h_im.reshape(n, gp)


def _ssm_step_kernel(x_ref, mod_ref, h0r_ref, h0i_ref, ar_ref, ai_ref, bbr_ref, bbi_ref, cr_ref, ci_ref, d_ref,
                     z_ref, hr_ref, hi_ref):
    u = x_ref[0] * (1.0 + mod_ref[0, 1]) + mod_ref[0, 0]
    ub = u.astype(BF16)
    nb, kin, kst = bbr_ref.shape
    ys = []
    for n in range(nb):
        cs = slice(n * kst, (n + 1) * kst)
        ubn = ub[:, n * kin:(n + 1) * kin]
        c_re, c_im = _cmul(ar_ref[:, cs], ai_ref[:, cs], h0r_ref[:, cs], h0i_ref[:, cs])
        h_re = jnp.dot(ubn, bbr_ref[n], preferred_element_type=F32) + c_re
        h_im = jnp.dot(ubn, bbi_ref[n], preferred_element_type=F32) + c_im
        hr_ref[:, cs] = h_re
        hi_ref[:, cs] = h_im
        ys.append(jnp.dot(h_re.astype(BF16), cr_ref[n], preferred_element_type=F32)
                  + jnp.dot(h_im.astype(BF16), ci_ref[n], preferred_element_type=F32))
    y = jnp.concatenate(ys, axis=1) + u * d_ref[...]
    z_ref[0] = _gelu_tanh(y).astype(BF16)


def _ssm_step_call(x, mod, h0_re, h0_im, sp, d_skip):
    _, rows, d = x.shape
    gp = sp["a_re"].shape[1]
    args = [x, mod, h0_re, h0_im, sp["a_re"], sp["a_im"], sp["bb_re"], sp["bb_im"], sp["c_re"], sp["c_im"],
            d_skip.reshape(1, d)]
    full = lambda a: pl.BlockSpec(a.shape, lambda: tuple(0 for _ in a.shape))
    out_shapes = [jax.ShapeDtypeStruct((1, rows, d), BF16),
                  jax.ShapeDtypeStruct((rows, gp), F32), jax.ShapeDtypeStruct((rows, gp), F32)]
    return pl.pallas_call(
        _ssm_step_kernel,
        in_specs=[full(a) for a in args],
        out_specs=[full(s) for s in out_shapes],
        out_shape=out_shapes,
        name="ssm_step",
        compiler_params=_cparams(None, 4 * sum(a.size * a.dtype.itemsize for a in args)),
    )(*args)


def _mixer_post_kernel(a_ref, x_ref, mod_ref, w_ref, lng_ref, lnb_ref, *rest, glu, router, alpha):
    if router:
        wr_ref, br_ref, x1_ref, h2_ref, lg_ref = rest
    else:
        x1_ref, h2_ref = rest
    proj = jnp.dot(a_ref[0], w_ref[...], preferred_element_type=F32)
    if glu:
        d = proj.shape[1] // 2
        proj = proj[:, :d] * _sigmoid(proj[:, d:])
    x1 = _post_norm(x_ref[0], proj, mod_ref[0, 2], lng_ref[...], lnb_ref[...], alpha)
    x1_ref[0] = x1
    h2 = x1 * (1.0 + mod_ref[0, 4]) + mod_ref[0, 3]
    h2_ref[0] = h2.astype(BF16)
    if router:
        nt = (((1,), (1,)), ((), ()))
        h_hi, h_lo = _split_bf16(h2, 2)
        w_hi, w_lo = _split_bf16(wr_ref[...], 2)
        lg = (lax.dot_general(w_hi, h_hi, nt, preferred_element_type=F32)
              + lax.dot_general(w_lo, h_hi, nt, preferred_element_type=F32)
              + lax.dot_general(w_hi, h_lo, nt, preferred_element_type=F32))
        lg_ref[...] = lg + br_ref[...]


def _mixer_post_call(a, x, mod, w, ln_g, ln_b, alpha, *, glu, w_router=None, b_router=None):
    n, t, d = x.shape
    tm = _row_tile(t, ROW_TILE)
    rmod = mod.shape[2]
    router = w_router is not None
    row = lambda width: pl.BlockSpec((1, tm, width), lambda i, j: (i, j, 0))
    const = lambda arr: pl.BlockSpec(arr.shape, lambda i, j: tuple(0 for _ in arr.shape))
    mod_spec = pl.BlockSpec((1, 6, rmod, d), (lambda i, j: (i, 0, 0, 0)) if rmod == 1 else (lambda i, j: (i, 0, j, 0)))
    args = [a, x, mod, w, ln_g.reshape(1, d), ln_b.reshape(1, d)]
    in_specs = [row(a.shape[2]), row(d), mod_spec, const(w), const(args[4]), const(args[5])]
    out_specs = [row(d), row(d)]
    out_shape = [jax.ShapeDtypeStruct((n, t, d), F32), jax.ShapeDtypeStruct((n, t, d), BF16)]
    if router:
        e = w_router.shape[1]
        args += [w_router.T, b_router.reshape(e, 1)]
        in_specs += [const(args[-2]), const(args[-1])]
        out_specs.append(pl.BlockSpec((e, tm), lambda i, j: (0, i * (t // tm) + j)))
        out_shape.append(jax.ShapeDtypeStruct((e, n * t), F32))
    vmem = 2 * w.size * 2 + 12 * tm * w.shape[1] * 4 + 8 * tm * d * 4
    return pl.pallas_call(
        functools.partial(_mixer_post_kernel, glu=glu, router=router, alpha=alpha),
        grid=(n, t // tm), in_specs=in_specs, out_specs=out_specs, out_shape=out_shape,
        name="mixer_post",
        compiler_params=_cparams(("parallel", "parallel"), vmem),
    )(*args)


def _swiglu_chunks(h, wg_ref, wu_ref, wd_ref, lead, fc):
    f = wg_ref.shape[-1]
    acc = None
    for lo in range(0, f, fc):
        hi = min(lo + fc, f)
        g = jnp.dot(h, wg_ref[lead + (slice(None), slice(lo, hi))], preferred_element_type=F32)
        u = jnp.dot(h, wu_ref[lead + (slice(None), slice(lo, hi))], preferred_element_type=F32)
        a = (g * _sigmoid(g) * u).astype(BF16)
        part = jnp.dot(a, wd_ref[lead + (slice(lo, hi), slice(None))], preferred_element_type=F32)
        acc = part if acc is None else acc + part
    return acc


def _ffn_post_kernel(h_ref, x_ref, mod_ref, wg_ref, wu_ref, wd_ref, lng_ref, lnb_ref, o_ref, *, alpha, fc):
    out = _swiglu_chunks(h_ref[0], wg_ref, wu_ref, wd_ref, (), fc)
    o_ref[0] = _post_norm(x_ref[0], out, mod_ref[0, 5], lng_ref[...], lnb_ref[...], alpha)


def _ffn_post_call(h, x, mod, wg, wu, wd, ln_g, ln_b, alpha):
    n, t, d = x.shape
    f = wg.shape[1]
    tm = _row_tile(t, ROW_TILE)
    rmod = mod.shape[2]
    row = lambda: pl.BlockSpec((1, tm, d), lambda i, j: (i, j, 0))
    const = lambda arr: pl.BlockSpec(arr.shape, lambda i, j: tuple(0 for _ in arr.shape),
                                     pipeline_mode=pl.Buffered(1))
    mod_spec = pl.BlockSpec((1, 6, rmod, d), (lambda i, j: (i, 0, 0, 0)) if rmod == 1 else (lambda i, j: (i, 0, j, 0)))
    fc = 2 * V7X_MXU_DIM
    args = [h, x, mod, wg, wu, wd, ln_g.reshape(1, d), ln_b.reshape(1, d)]
    vmem = 3 * d * f * 2 + 10 * tm * d * 4 + 6 * tm * fc * 4
    return pl.pallas_call(
        functools.partial(_ffn_post_kernel, alpha=alpha, fc=fc),
        grid=(n, t // tm),
        in_specs=[row(), row(), mod_spec] + [const(a) for a in args[3:]],
        out_specs=row(),
        out_shape=jax.ShapeDtypeStruct((n, t, d), F32), name="ffn_post",
        compiler_params=_cparams(("parallel", "parallel"), vmem),
    )(*args)


MOE_DISPATCH_TILE = 512
MOE_ROW_TILE = 512
MOE_COMBINE_TILE = 128
BF16_ROWS = 16


def _route_kernel(lg_ref, upper_ref, gate_ref, sel_ref, rank_ref, cnt_ref):
    @pl.when(pl.program_id(0) == 0)
    def _():
        cnt_ref[...] = jnp.zeros_like(cnt_ref)

    lg = lg_ref[...]
    e = lg.shape[0]
    idx = lax.broadcasted_iota(jnp.int32, lg.shape, 0)
    m1 = jnp.max(lg, axis=0, keepdims=True)
    i1 = jnp.min(jnp.where(lg == m1, idx, e), axis=0, keepdims=True)
    rest = jnp.where(idx == i1, -jnp.inf, lg)
    m2 = jnp.max(rest, axis=0, keepdims=True)
    i2 = jnp.min(jnp.where(rest == m2, idx, e), axis=0, keepdims=True)
    e2 = jnp.exp(m2 - m1)
    den = 1.0 + e2
    gate_ref[...] = jnp.where(idx == i1, 1.0 / den, 0.0) + jnp.where(idx == i2, e2 / den, 0.0)
    sel = jnp.where(idx == i1, 1.0, 0.0) + jnp.where(idx == i2, 1.0, 0.0)
    sel_ref[...] = sel
    rank_ref[...] = cnt_ref[...] + jnp.dot(sel.astype(BF16), upper_ref[...], preferred_element_type=F32)
    cnt_ref[...] += jnp.sum(sel, axis=1, keepdims=True)


def _route_call(logits_t):
    e, m = logits_t.shape
    tr = _row_tile(m, ROW_TILE)
    upper = jnp.triu(jnp.ones((tr, tr), F32), k=1).astype(BF16)
    blk = pl.BlockSpec((e, tr), lambda i: (0, i))
    out_shape = [jax.ShapeDtypeStruct((e, m), F32)] * 3 + [jax.ShapeDtypeStruct((e, 1), F32)]
    return pl.pallas_call(
        _route_kernel, grid=(m // tr,),
        in_specs=[blk, pl.BlockSpec((tr, tr), lambda i: (0, 0))],
        out_specs=[blk, blk, blk, pl.BlockSpec((e, 1), lambda i: (0, 0))],
        out_shape=out_shape, name="moe_route",
        compiler_params=_cparams(("arbitrary",), 4 * tr * tr * 2 + 64 * e * tr * 4),
    )(logits_t, upper)


def _moe_schedule(gate_t, sel_t, rank_t, cnt, tt, tme, tc):
    e, m = sel_t.shape
    cnt_i = cnt[:, 0].astype(jnp.int32)
    seg_tiles = cnt_i // tme + 1
    seg_end = jnp.cumsum(seg_tiles)
    seg_start = seg_end - seg_tiles
    offsets = seg_start * tme
    n_tiles = (TOP_K * m) // tme + e
    tile_ids = jnp.arange(n_tiles, dtype=jnp.int32)
    tile_expert = jnp.minimum(jnp.sum(tile_ids[:, None] >= seg_end[None, :], axis=1), e - 1).astype(jnp.int32)
    tile_valid = (tile_ids < seg_end[-1]).astype(jnp.int32)
    rank_i = rank_t.astype(jnp.int32)
    rank_at = jnp.concatenate([rank_i[:, ::tt], cnt_i[:, None]], axis=1)
    done_tiles = rank_at // tme
    w0 = (rank_at - done_tiles * tme) // BF16_ROWS * BF16_ROWS
    completes = jnp.concatenate([rank_at[:, 1:] >= (done_tiles[:, :-1] + 1) * tme,
                                 jnp.ones((e, 1), bool)], axis=1).astype(jnp.int32)
    fits = rank_at[:, 1:] - rank_at[:, :-1] + BF16_ROWS <= _dispatch_narrow_window(tt)
    completes = completes + 2 * jnp.concatenate([fits, jnp.zeros((e, 1), bool)], axis=1).astype(jnp.int32)
    pos = jnp.where(sel_t > 0, offsets[:, None].astype(F32) + rank_t, -1.0).T
    blk0 = ((offsets[:, None] + rank_i[:, ::tc]) // tc).T
    return dict(base=(done_tiles * tme + w0).reshape(-1), w0=w0.reshape(-1), completes=completes.reshape(-1),
                out_tile=(seg_start[:, None] + done_tiles).reshape(-1), tile_expert=tile_expert,
                tile_valid=tile_valid, n_tiles=n_tiles, blk0=blk0.reshape(-1).astype(jnp.int32), pos=pos,
                gates=gate_t.T)


def _dispatch_narrow_window(tt):
    half = tt // 2
    return half if half % BF16_ROWS == 0 and half > BF16_ROWS else tt + BF16_ROWS


def _dispatch_kernel(base_ref, w0_ref, done_ref, tile_ref, h_ref, sel_ref, rank_ref, o_ref, stage_ref, *, tme):
    del tile_ref
    ex = pl.program_id(0)
    j = pl.program_id(1)
    steps = pl.num_programs(1)
    flat = ex * steps + j
    tt = h_ref.shape[0]
    win = tt + BF16_ROWS

    @pl.when(j == 0)
    def _():
        stage_ref[...] = jnp.zeros_like(stage_ref)

    def place(rows_in_window):
        w0 = pl.multiple_of(w0_ref[flat], BF16_ROWS)
        local = rank_ref[pl.ds(ex, 1), :] - base_ref[flat].astype(F32)
        local = jnp.where(sel_ref[pl.ds(ex, 1), :] > 0, local, -1.0)
        rows = lax.broadcasted_iota(jnp.int32, (rows_in_window, tt), 0).astype(F32)
        onehot = jnp.where(rows == local, 1.0, 0.0).astype(BF16)
        stage_ref[pl.ds(w0, rows_in_window), :] += jnp.dot(onehot, h_ref[...], preferred_element_type=F32)

    narrow = _dispatch_narrow_window(tt)
    flag = done_ref[flat]

    @pl.when((j < steps - 1) & (flag >= 2))
    def _():
        place(narrow)

    @pl.when((j < steps - 1) & (flag < 2))
    def _():
        place(win)

    @pl.when(flag % 2 == 1)
    def _():
        o_ref[...] = stage_ref[0:tme, :].astype(BF16)
        tail = stage_ref[tme:, :]
        stage_ref[...] = jnp.zeros_like(stage_ref)
        stage_ref[0:win, :] = tail


def _dispatch_call(h, sel_t, rank_t, sched, tt, tme):
    m, d = h.shape
    e = sel_t.shape[0]
    nj = m // tt
    steps = nj + 1
    clamp = lambda j: jnp.minimum(j, nj - 1)
    grid_spec = pltpu.PrefetchScalarGridSpec(
        num_scalar_prefetch=4, grid=(e, steps),
        in_specs=[pl.BlockSpec((tt, d), lambda ex, j, *_: (clamp(j), 0)),
                  pl.BlockSpec((e, tt), lambda ex, j, *_: (0, clamp(j))),
                  pl.BlockSpec((e, tt), lambda ex, j, *_: (0, clamp(j)))],
        out_specs=pl.BlockSpec((tme, d), lambda ex, j, base, w0, done, tile: (tile[ex * steps + j], 0)),
        scratch_shapes=[pltpu.VMEM((tme + tt + BF16_ROWS, d), F32)])
    return pl.pallas_call(
        functools.partial(_dispatch_kernel, tme=tme), grid_spec=grid_spec,
        out_shape=jax.ShapeDtypeStruct((sched["n_tiles"] * tme, d), BF16), name="moe_dispatch",
        compiler_params=_cparams(("arbitrary", "arbitrary"), 12 * (tme + tt) * d * 4),
    )(sched["base"], sched["w0"], sched["completes"], sched["out_tile"], h, sel_t, rank_t)


def _group_ffn_kernel(texp_ref, valid_ref, x_ref, wg_ref, wu_ref, wd_ref, y_ref, *, fc):
    del texp_ref
    live = valid_ref[pl.program_id(0)] == 1

    @pl.when(live)
    def _():
        y_ref[...] = _swiglu_chunks(x_ref[...], wg_ref, wu_ref, wd_ref, (0, 0), fc).astype(BF16)

    @pl.when(jnp.logical_not(live))
    def _():
        y_ref[...] = jnp.zeros_like(y_ref)


def _group_ffn_call(xs, sched, wg, wu, wd, layer, tme):
    rows, d = xs.shape
    f = wg.shape[3]
    fc = 2 * V7X_MXU_DIM
    row = pl.BlockSpec((tme, d), lambda i, texp, valid: (i, 0))
    grid_spec = pltpu.PrefetchScalarGridSpec(
        num_scalar_prefetch=2, grid=(rows // tme,),
        in_specs=[row,
                  pl.BlockSpec((1, 1, d, f), lambda i, texp, valid: (layer, texp[i], 0, 0)),
                  pl.BlockSpec((1, 1, d, f), lambda i, texp, valid: (layer, texp[i], 0, 0)),
                  pl.BlockSpec((1, 1, f, d), lambda i, texp, valid: (layer, texp[i], 0, 0))],
        out_specs=row)
    return pl.pallas_call(
        functools.partial(_group_ffn_kernel, fc=fc), grid_spec=grid_spec,
        out_shape=jax.ShapeDtypeStruct((rows, d), BF16), name="moe_group_ffn",
        compiler_params=_cparams(("arbitrary",), 2 * 3 * d * f * 2 + 8 * tme * d * 4 + 6 * tme * fc * 4),
    )(sched["tile_expert"], sched["tile_valid"], xs, wg, wu, wd)


def _combine_post_kernel(blk_ref, x_ref, mod_ref, pos_ref, gate_ref, lng_ref, lnb_ref, *rest, experts, alpha):
    y_refs, o_ref = rest[:2 * experts], rest[2 * experts]
    tc = x_ref.shape[1]
    tile = pl.program_id(0) * pl.num_programs(1) + pl.program_id(1)
    lane = lax.broadcasted_iota(jnp.int32, (tc, 2 * tc), 1).astype(F32)
    pos = pos_ref[...]
    gates = gate_ref[...]
    acc = jnp.zeros(x_ref.shape[1:], F32)
    for e in range(experts):
        base = (blk_ref[tile * experts + e] * tc).astype(F32)
        onehot = jnp.where(lane == pos[:, e:e + 1] - base, 1.0, 0.0).astype(BF16)
        window = jnp.concatenate([y_refs[2 * e][...], y_refs[2 * e + 1][...]], axis=0)
        acc = acc + gates[:, e:e + 1] * jnp.dot(onehot, window, preferred_element_type=F32)
    o_ref[0] = _post_norm(x_ref[0], acc, mod_ref[0, 5], lng_ref[...], lnb_ref[...], alpha)


def _combine_post_call(ys, x, mod, sched, ln_g, ln_b, alpha, tc):
    n, t, d = x.shape
    e = sched["pos"].shape[1]
    nj = t // tc
    nblk = ys.shape[0] // tc
    rmod = mod.shape[2]
    row = pl.BlockSpec((1, tc, d), lambda i, j, blk: (i, j, 0))
    per_tok = pl.BlockSpec((tc, e), lambda i, j, blk: (i * nj + j, 0))
    const = lambda arr: pl.BlockSpec(arr.shape, lambda i, j, blk: tuple(0 for _ in arr.shape))
    mod_spec = pl.BlockSpec((1, 6, rmod, d),
                            (lambda i, j, blk: (i, 0, 0, 0)) if rmod == 1 else (lambda i, j, blk: (i, 0, j, 0)))

    def gather_spec(ex, half):
        return pl.BlockSpec((tc, d), lambda i, j, blk: (jnp.minimum(blk[(i * nj + j) * e + ex] + half, nblk - 1), 0))

    lng, lnb = ln_g.reshape(1, d), ln_b.reshape(1, d)
    grid_spec = pltpu.PrefetchScalarGridSpec(
        num_scalar_prefetch=1, grid=(n, nj),
        in_specs=[row, mod_spec, per_tok, per_tok, const(lng), const(lnb)]
                 + [gather_spec(ex, half) for ex in range(e) for half in range(2)],
        out_specs=row)
    return pl.pallas_call(
        functools.partial(_combine_post_kernel, experts=e, alpha=alpha), grid_spec=grid_spec,
        out_shape=jax.ShapeDtypeStruct((n, t, d), F32), name="moe_combine_post",
        compiler_params=_cparams(("parallel", "parallel"), 4 * 2 * e * tc * d * 2 + 16 * tc * d * 4),
    )(sched["blk0"], x, mod, sched["pos"], sched["gates"], lng, lnb, *([ys] * (2 * e)))


def _moe_call(h2, logits_t, x1, mod, wg, wu, wd, layer, ln_g, ln_b, alpha):
    n, t, d = x1.shape
    m = n * t
    tt = min(MOE_DISPATCH_TILE, m)
    tme = min(MOE_ROW_TILE, m)
    tc = min(MOE_COMBINE_TILE, t)
    gate_t, sel_t, rank_t, cnt = _route_call(logits_t)
    sched = _moe_schedule(gate_t, sel_t, rank_t, cnt, tt, tme, tc)
    xs = _dispatch_call(h2.reshape(m, d), sel_t, rank_t, sched, tt, tme)
    ys = _group_ffn_call(xs, sched, wg, wu, wd, layer, tme)
    return _combine_post_call(ys, x1, mod, sched, ln_g, ln_b, alpha, tc)


def _log_sigmoid(z):
    return -(jnp.maximum(-z, 0.0) + jnp.log1p(jnp.exp(-jnp.abs(z))))


ATTN_LANES = V7X_LANES
AUG_TERMS = 3
LOG2E = 1.4426950408889634


def _aug_selectors(heads, hd, offset):
    rows = jnp.arange(AUG_TERMS * heads)
    col = (rows % heads) * ATTN_LANES + hd + offset + rows // heads
    return jnp.zeros((AUG_TERMS * heads, heads * ATTN_LANES), F32).at[rows, col].set(1.0).astype(BF16)


def _aug_ones(heads, hd, offset, count):
    lane = jnp.arange(heads * ATTN_LANES) % ATTN_LANES
    return ((lane >= hd + offset) & (lane < hd + offset + count)).astype(F32).reshape(1, heads * ATTN_LANES)


def _place_terms(x, sel_ref, sign):
    terms = jnp.concatenate([t.astype(F32) for t in _split_bf16(x, AUG_TERMS)], axis=1).astype(BF16)
    return sign * jnp.dot(terms, sel_ref[...], preferred_element_type=F32)


def _store_head_operands(dense, tail, out_ref, hd):
    per_tile = ATTN_LANES // hd
    own = lax.broadcasted_iota(jnp.int32, (dense.shape[0], ATTN_LANES), 1) < hd
    for h in range(out_ref.shape[1]):
        tile = dense[:, (h // per_tile) * ATTN_LANES:(h // per_tile + 1) * ATTN_LANES]
        if h % per_tile:
            tile = pltpu.roll(tile, ATTN_LANES - (h % per_tile) * hd, axis=1)
        out_ref[0, h] = jnp.where(own, tile, tail[:, h * ATTN_LANES:(h + 1) * ATTN_LANES]).astype(BF16)


def _kv_kernel(x_ref, wk_ref, wv_ref, wf_ref, bf_ref, *rest, cumulative):
    if cumulative:
        tri_ref, sel_ref, kone_ref, vone_ref, k_ref, v_ref, lf_ref, fk_ref, ka_ref, va_ref, carry_ref = rest
    else:
        k_ref, v_ref, lf_ref = rest
    xb = x_ref[0].astype(BF16)
    k = jnp.dot(xb, wk_ref[...], preferred_element_type=F32)
    v = jnp.dot(xb, wv_ref[...], preferred_element_type=F32)
    k_ref[0] = k
    v_ref[0] = v
    lf = _log_sigmoid(jnp.dot(xb, wf_ref[...], preferred_element_type=F32) + bf_ref[...])
    lf_ref[0] = lf
    if cumulative:
        @pl.when(pl.program_id(1) == 0)
        def _():
            carry_ref[...] = jnp.zeros_like(carry_ref)

        fk = carry_ref[...] + jnp.dot(tri_ref[...], jnp.concatenate(_split_bf16(lf, 3), axis=0),
                                      preferred_element_type=F32)
        fk_ref[0] = fk
        carry_ref[...] = fk[fk.shape[0] - 1:]
        hd = k.shape[1] // ka_ref.shape[1]
        _store_head_operands(k, kone_ref[...] + _place_terms(fk * LOG2E, sel_ref, -1.0), ka_ref, hd)
        _store_head_operands(v, jnp.broadcast_to(vone_ref[...], (v.shape[0], vone_ref.shape[1])), va_ref, hd)


def _kv_call(x, wk, wv, wf, bf, *, cumulative):
    n, t, d = x.shape
    hh = wf.shape[1]
    hd = d // hh
    tm = _row_tile(t, ROW_TILE)
    row = lambda width: pl.BlockSpec((1, tm, width), lambda i, j: (i, j, 0))
    const = lambda arr: pl.BlockSpec(arr.shape, lambda i, j: tuple(0 for _ in arr.shape))
    args = [x, wk, wv, wf, bf.reshape(1, hh)]
    out_specs = [row(d), row(d), row(hh)]
    out_shape = [jax.ShapeDtypeStruct((n, t, d), F32)] * 2 + [jax.ShapeDtypeStruct((n, t, hh), F32)]
    scratch = []
    if cumulative:
        tri = jnp.tril(jnp.ones((tm, tm), F32)).astype(BF16)
        args += [jnp.concatenate([tri] * 3, axis=1),
                 _aug_selectors(hh, hd, AUG_TERMS), _aug_ones(hh, hd, 0, AUG_TERMS), _aug_ones(hh, hd, 0, 1)]
        head_rows = pl.BlockSpec((1, hh, tm, ATTN_LANES), lambda i, j: (i, 0, j, 0))
        out_specs += [row(hh), head_rows, head_rows]
        out_shape += [jax.ShapeDtypeStruct((n, t, hh), F32)] + [jax.ShapeDtypeStruct((n, hh, t, ATTN_LANES), BF16)] * 2
        scratch = [pltpu.VMEM((1, hh), F32)]
    return pl.pallas_call(
        functools.partial(_kv_kernel, cumulative=cumulative),
        grid=(n, t // tm),
        in_specs=[row(d)] + [const(a) for a in args[1:]],
        out_specs=out_specs, out_shape=out_shape, scratch_shapes=scratch, name="kv_proj",
        compiler_params=_cparams(("parallel", "arbitrary"), 16 * d * d + 40 * tm * d * 4),
    )(*args)


def _q_kernel(x_ref, mod_ref, w_ref, *rest, scale, augmented):
    h = (x_ref[0] * (1.0 + mod_ref[0, 1]) + mod_ref[0, 0]).astype(BF16)
    q = jnp.dot(h, w_ref[...], preferred_element_type=F32) * scale
    if augmented:
        fq_ref, sel_ref, one_ref, q_ref = rest
        _store_head_operands(q, one_ref[...] + _place_terms(fq_ref[0] * LOG2E, sel_ref, 1.0), q_ref,
                             q.shape[1] // q_ref.shape[1])
    else:
        q_ref, = rest
        q_ref[0] = q.astype(BF16)


def _q_call(x, mod, wq, heads, fq=None):
    n, t, d = x.shape
    hd = d // heads
    tm = _row_tile(t, ROW_TILE)
    rmod = mod.shape[2]
    row = lambda width: pl.BlockSpec((1, tm, width), lambda i, j: (i, j, 0))
    const = lambda arr: pl.BlockSpec(arr.shape, lambda i, j: tuple(0 for _ in arr.shape))
    mod_spec = pl.BlockSpec((1, 6, rmod, d), (lambda i, j: (i, 0, 0, 0)) if rmod == 1 else (lambda i, j: (i, 0, j, 0)))
    augmented = fq is not None
    if augmented:
        args = [x, mod, wq, fq, _aug_selectors(heads, hd, 0),
                _aug_ones(heads, hd, AUG_TERMS, AUG_TERMS)]
        in_specs = [row(d), mod_spec, const(args[2]), row(heads), const(args[4]), const(args[5])]
        out_specs = pl.BlockSpec((1, heads, tm, ATTN_LANES), lambda i, j: (i, 0, j, 0))
        out_shape = jax.ShapeDtypeStruct((n, heads, t, ATTN_LANES), BF16)
        scale = hd ** -0.5 * LOG2E
    else:
        args = [x, mod, wq]
        in_specs = [row(d), mod_spec, const(wq)]
        out_specs = row(d)
        out_shape = jax.ShapeDtypeStruct((n, t, d), BF16)
        scale = hd ** -0.5
    return pl.pallas_call(
        functools.partial(_q_kernel, scale=scale, augmented=augmented),
        grid=(n, t // tm), in_specs=in_specs, out_specs=out_specs, out_shape=out_shape, name="q_proj",
        compiler_params=_cparams(("parallel", "parallel"), 8 * d * d + 24 * tm * d * 4),
    )(*args)


ATTN_ROW_BLOCK = 128
ATTN_HEAD_GROUP = 16


def _attn_kernel(q_ref, k_ref, v_ref, o_ref, m_ref, acc_ref, *, hd, ratio):
    qi = pl.program_id(1)
    kj = pl.program_id(2)
    heads, tq = q_ref.shape[1], q_ref.shape[2]
    tk = k_ref.shape[2]
    lanes = acc_ref.shape[2]
    rb = min(ATTN_ROW_BLOCK, tq)
    group = math.gcd(heads, ATTN_HEAD_GROUP)
    nt = (((1,), (1,)), ((), ()))
    first_diag = qi * ratio

    @pl.when(kj == 0)
    def _():
        m_ref[...] = jnp.full_like(m_ref, NEG_INF)
        acc_ref[...] = jnp.zeros_like(acc_ref)

    def sweep(diagonal):
        blocks = [slice(r * rb, (r + 1) * rb) for r in range(tq // rb)]
        shift = (kj - first_diag) * tk

        def group_body(g, carry):
            hs = [g * group + i for i in range(group)]
            scores = [[lax.dot_general(q_ref[0, h, rows, :], k_ref[0, h], nt, preferred_element_type=F32)
                       for rows in blocks] for h in hs]
            for h, s_h in zip(hs, scores):
                vh = v_ref[0, h]
                m_all = m_ref[h]
                acc_all = acc_ref[h]
                m_out, acc_out = [], []
                for r, (rows, s) in enumerate(zip(blocks, s_h)):
                    if diagonal:
                        qpos = r * rb + lax.broadcasted_iota(jnp.int32, (rb, tk), 0)
                        kpos = shift + lax.broadcasted_iota(jnp.int32, (rb, tk), 1)
                        s = jnp.where(kpos <= qpos, s, NEG_INF)
                    m_prev = m_all[rows]
                    m_new = jnp.maximum(m_prev, jnp.max(s, axis=-1, keepdims=True))
                    p = jnp.exp2(s - jnp.concatenate([m_new] * (tk // lanes), axis=1))
                    acc_out.append(jnp.exp2(m_prev - m_new) * acc_all[rows]
                                   + jnp.dot(p.astype(BF16), vh, preferred_element_type=F32))
                    m_out.append(m_new)
                m_ref[h] = jnp.concatenate(m_out, axis=0)
                acc_ref[h] = jnp.concatenate(acc_out, axis=0)
            return carry
        lax.fori_loop(0, heads // group, group_body, 0)

    @pl.when(kj < first_diag)
    def _():
        sweep(False)

    @pl.when((kj >= first_diag) & (kj < first_diag + ratio))
    def _():
        sweep(True)

    @pl.when(kj == first_diag + ratio - 1)
    def _():
        for h in range(heads):
            a = acc_ref[h]
            o_ref[0, :, h * hd:(h + 1) * hd] = (a[:, :hd] / a[:, hd:hd + 1]).astype(BF16)


def _attn_call(qa, ka, va, hd):
    n, heads, t, lanes = qa.shape
    tk = _row_tile(t, ATTN_TILE)
    tq = _row_tile(t, ATTN_QUERY_TILE)
    ratio = tq // tk
    assert tq == ratio * tk and tk % lanes == 0
    kv_spec = pl.BlockSpec((1, heads, tk, lanes),
                           lambda i, a, b: (i, 0, jnp.minimum(b, (a + 1) * ratio - 1), 0))
    group = math.gcd(heads, ATTN_HEAD_GROUP)
    return pl.pallas_call(
        functools.partial(_attn_kernel, hd=hd, ratio=ratio),
        grid=(n, t // tq, t // tk),
        in_specs=[pl.BlockSpec((1, heads, tq, lanes), lambda i, a, b: (i, 0, a, 0)), kv_spec, kv_spec],
        out_specs=pl.BlockSpec((1, tq, heads * hd), lambda i, a, b: (i, a, 0)),
        out_shape=jax.ShapeDtypeStruct((n, t, heads * hd), BF16),
        scratch_shapes=[pltpu.VMEM((heads, tq, lanes), F32), pltpu.VMEM((heads, tq, lanes), F32)],
        name="causal_attn",
        compiler_params=_cparams(("parallel", "parallel", "arbitrary"),
                                 heads * tq * lanes * (2 * 4 + 2 * 2) + 2 * 2 * heads * tk * lanes * 2
                                 + 2 * tq * heads * hd * 2 + 3 * group * tq * tk * 4),
    )(qa, ka, va)


def _column_to_row(col, eye):
    return jnp.sum(jnp.where(eye > 0, col, 0.0), axis=0, keepdims=True)


DECODE_PAGES_PER_STEP = 8


def _decode_attn_kernel(pt_ref, qb_ref, *rest, g):
    del pt_ref
    kt_refs, vt_refs, lft_refs = rest[:g], rest[g:2 * g], rest[2 * g:3 * g]
    (knew_ref, vnew_ref, lfnew_ref, later_ref, diag_ref, eye_ref, o_ref, m_ref, l_ref, acc_ref, suf_ref) = rest[3 * g:]
    step = pl.program_id(1)
    qb = qb_ref[0]
    eye = eye_ref[...]

    @pl.when(step == 0)
    def _():
        m_ref[...] = jnp.dot(qb, knew_ref[0].astype(BF16), preferred_element_type=F32)
        l_ref[...] = jnp.ones_like(l_ref)
        acc_ref[...] = jnp.broadcast_to(vnew_ref[0], acc_ref.shape)
        suf_ref[...] = lfnew_ref[0]

    scores = []
    after = suf_ref[...]
    for kt_ref, lft_ref in zip(kt_refs, lft_refs):
        lf = lft_ref[0]
        lf3 = jnp.concatenate(_split_bf16(lf, 3), axis=1)
        scores.append(jnp.dot(qb, kt_ref[0].astype(BF16), preferred_element_type=F32)
                      + jnp.dot(lf3, later_ref[...], preferred_element_type=F32) + after)
        after = after + jnp.sum(lf, axis=1, keepdims=True)
    suf_ref[...] = after
    m_prev = m_ref[...]
    m_new = m_prev
    for s in scores:
        m_new = jnp.maximum(m_new, jnp.max(s, axis=1, keepdims=True))
    alpha = jnp.exp(m_prev - m_new)
    l_new = alpha * l_ref[...]
    acc = _column_to_row(alpha, eye) * acc_ref[...]
    for s, vt_ref in zip(scores, vt_refs):
        p = jnp.exp(s - m_new)
        l_new = l_new + jnp.sum(p, axis=1, keepdims=True)
        acc = acc + lax.dot_general(vt_ref[0].astype(BF16), p.astype(BF16), (((1,), (1,)), ((), ())),
                                    preferred_element_type=F32)
    l_ref[...] = l_new
    acc_ref[...] = acc
    m_ref[...] = m_new

    @pl.when(step == pl.num_programs(1) - 1)
    def _():
        on_diag = diag_ref[...] > 0
        num = jnp.sum(jnp.where(on_diag, acc_ref[...], 0.0), axis=1, keepdims=True)
        den = jnp.sum(jnp.where(on_diag, _column_to_row(l_ref[...], eye), 0.0), axis=1, keepdims=True)
        o_ref[0] = (num / den).astype(BF16)


def _decode_attn_call(page_table, q, cache_k, cache_v, cache_logf, k_new, v_new, lf_new, heads):
    nseq, npages = page_table.shape
    n_phys, page, _, hd = cache_k.shape
    d = heads * hd
    seg = jnp.repeat(jnp.eye(heads, dtype=F32), hd, axis=1)
    qb = q[:, None, :] * seg[None].astype(BF16)
    later = jnp.tril(jnp.ones((page, page), F32), k=-1).astype(BF16)
    later3 = jnp.concatenate([later] * 3, axis=0)
    kt = cache_k.transpose(0, 2, 3, 1).reshape(n_phys, d, page)
    vt = cache_v.transpose(0, 2, 3, 1).reshape(n_phys, d, page)
    lft = cache_logf.transpose(0, 2, 1)
    g = math.gcd(npages, DECODE_PAGES_PER_STEP)

    def paged(rows):
        return [pl.BlockSpec((1, rows, page), functools.partial(
            lambda b, p, pt, i: (pt[b, npages - 1 - (p * g + i)], 0, 0), i=i)) for i in range(g)]

    per_seq = lambda arr: pl.BlockSpec((1,) + arr.shape[1:], lambda b, p, pt: (b,) + tuple(0 for _ in arr.shape[1:]))
    const = lambda arr: pl.BlockSpec(arr.shape, lambda b, p, pt: tuple(0 for _ in arr.shape))
    tail = [k_new.reshape(nseq, d, 1), v_new.reshape(nseq, d, 1), lf_new.reshape(nseq, heads, 1),
            later3, seg.T, jnp.eye(heads, dtype=F32)]
    args = [qb] + [kt] * g + [vt] * g + [lft] * g + tail
    grid_spec = pltpu.PrefetchScalarGridSpec(
        num_scalar_prefetch=1, grid=(nseq, npages // g),
        in_specs=[per_seq(qb)] + paged(d) + paged(d) + paged(heads)
                 + [per_seq(a) for a in tail[:3]] + [const(a) for a in tail[3:]],
        out_specs=pl.BlockSpec((1, d, 1), lambda b, p, pt: (b, 0, 0)),
        scratch_shapes=[pltpu.VMEM((heads, 1), F32), pltpu.VMEM((heads, 1), F32), pltpu.VMEM((d, heads), F32),
                        pltpu.VMEM((heads, 1), F32)])
    out = pl.pallas_call(
        functools.partial(_decode_attn_kernel, g=g), grid_spec=grid_spec,
        out_shape=jax.ShapeDtypeStruct((nseq, d, 1), BF16), name="paged_decode_attn",
        compiler_params=_cparams(("parallel", "arbitrary"), (8 + 7 * g) * page * d * 4),
    )(page_table, *args)
    return out.reshape(1, nseq, d)


def _trunk(x, mods, h0_re, h0_im, wts, ssm, *, sequence, paged=None):
    depth = mods.shape[0]
    n_a = wts["w_glu"].shape[0]
    heads = wts["w_f"].shape[1]
    nb, rows, d = x.shape
    hd = d // heads
    alpha = (2.0 * depth) ** 0.25
    new_re, new_im = [], []
    k_new = v_new = lf_new = None
    fk = k_aug = v_aug = None
    for l in range(depth):
        mod = mods[l]
        moe = l % 2 == 1
        li = l // 2
        router = dict(w_router=wts["w_router"][li], b_router=wts["b_router"][li]) if moe else {}
        if l < n_a:
            if sequence:
                z, hr, hi = _ssm_seq_call(x, mod, h0_re[l], h0_im[l], ssm[l], wts["ssm_d"][l])
            else:
                z, hr, hi = _ssm_step_call(x, mod, h0_re[l], h0_im[l], ssm[l], wts["ssm_d"][l])
            new_re.append(hr)
            new_im.append(hi)
            res = _mixer_post_call(z, x, mod, wts["w_glu"][l], wts["ln_g"][l, 0], wts["ln_b"][l, 0], alpha,
                                   glu=True, **router)
        else:
            lb = l - n_a
            if sequence:
                o = _attn_call(_q_call(x, mod, wts["w_q"][lb], heads, fq=fk), k_aug, v_aug, hd)
            else:
                q = _q_call(x, mod, wts["w_q"][lb], heads)
                o = _decode_attn_call(paged[0], q[0], paged[1], paged[2], paged[3], k_new[0], v_new[0], lf_new[0],
                                      heads)
            res = _mixer_post_call(o, x, mod, wts["w_o"][lb], wts["ln_g"][l, 0], wts["ln_b"][l, 0], alpha,
                                   glu=False, **router)
        if moe:
            x1, h2, logits_t = res
            x = _moe_call(h2, logits_t, x1, mod, wts["w_exp_gate"], wts["w_exp_up"], wts["w_exp_down"], li,
                          wts["ln_g"][l, 1], wts["ln_b"][l, 1], alpha)
        else:
            x1, h2 = res
            x = _ffn_post_call(h2, x1, mod, wts["w_ff_gate"][li], wts["w_ff_up"][li], wts["w_ff_down"][li],
                               wts["ln_g"][l, 1], wts["ln_b"][l, 1], alpha)
        if l == n_a - 1:
            outs = _kv_call(x, wts["w_k"], wts["w_v"], wts["w_f"], wts["b_f"], cumulative=sequence)
            k_new, v_new, lf_new = outs[:3]
            if sequence:
                fk, k_aug, v_aug = outs[3:]
    return x, jnp.stack(new_re), jnp.stack(new_im), k_new, v_new, lf_new


def kernel(x_prompt, x_sample, c_prompt, c_sample, cache_k, cache_v, cache_logf, state_ssm_re, state_ssm_im,
           page_table, w_mod, b_mod, ln_g, ln_b, ssm_lam_re, ssm_lam_im, ssm_log_dt, ssm_b_re, ssm_b_im,
           ssm_c_re, ssm_c_im, ssm_d, w_glu, w_k, w_v, w_f, b_f, w_q, w_o, w_ff_gate, w_ff_up, w_ff_down,
           w_router, b_router, w_exp_gate, w_exp_up, w_exp_down):
    batch, seq, d = x_prompt.shape
    dec_batch, dec_seq, _ = x_sample.shape
    assert dec_seq == 1
    depth = w_mod.shape[0]
    n_a, g, p = ssm_lam_re.shape
    heads = w_f.shape[1]
    hd = d // heads
    gp = g * p

    n_c = batch + dec_batch
    c_all = jnp.concatenate([c_prompt, c_sample], axis=0)
    c_all = jnp.pad(c_all, ((0, (-n_c) % 8), (0, 0)))
    mods = _mod_call(c_all, w_mod, b_mod)
    mods_p = mods[:, :batch].reshape(depth, batch, 6, 1, d)
    mods_s = mods[:, batch:n_c].reshape(depth, dec_batch, 6, d).transpose(0, 2, 1, 3)[:, None]

    bf = lambda w: w.astype(BF16)
    wts = dict(ln_g=ln_g, ln_b=ln_b, ssm_d=ssm_d, w_glu=bf(w_glu), w_k=bf(w_k), w_v=bf(w_v), w_f=bf(w_f), b_f=b_f,
               w_q=bf(w_q), w_o=bf(w_o), w_ff_gate=bf(w_ff_gate), w_ff_up=bf(w_ff_up), w_ff_down=bf(w_ff_down),
               w_router=w_router, b_router=b_router, w_exp_gate=bf(w_exp_gate), w_exp_up=bf(w_exp_up),
               w_exp_down=bf(w_exp_down))
    ssm = [_ssm_prep(ssm_lam_re[l], ssm_lam_im[l], ssm_log_dt[l], ssm_b_re[l], ssm_b_im[l], ssm_c_re[l],
                     ssm_c_im[l]) for l in range(n_a)]

    h0 = jnp.zeros((n_a, batch, gp), F32)
    y_p, re_p, im_p, k_p, v_p, lf_p = _trunk(x_prompt, mods_p, h0, h0, wts, ssm, sequence=True)

    x_s = x_sample.reshape(1, dec_batch, d)
    y_s, re_s, im_s, k_s, v_s, lf_s = _trunk(
        x_s, mods_s, state_ssm_re.reshape(n_a, dec_batch, gp), state_ssm_im.reshape(n_a, dec_batch, gp), wts, ssm,
        sequence=False, paged=(page_table, cache_k, cache_v, cache_logf))

    return (y_p, y_s.reshape(dec_batch, 1, d),
            re_p.reshape(n_a, batch, g, p), im_p.reshape(n_a, batch, g, p),
            k_p.reshape(batch, seq, heads, hd), v_p.reshape(batch, seq, heads, hd), lf_p,
            re_s.reshape(n_a, dec_batch, g, p), im_s.reshape(n_a, dec_batch, g, p),
            k_s.reshape(dec_batch, 1, heads, hd), v_s.reshape(dec_batch, 1, heads, hd),
            lf_s.reshape(dec_batch, 1, heads))
```

```python
import functools
import math

import jax
import jax.numpy as jnp
from jax import lax
from jax.experimental import pallas as pl
from jax.experimental.pallas import tpu as pltpu

F32 = jnp.float32
BF16 = jnp.bfloat16

GROUP_SIZE = 16
TOP_K = 2
LN_EPS = 1e-5
NEG_INF = -1e30

V7X_MXU_DIM = 256
V7X_LANES = 128
V7X_SUBLANES = 8
V7X_VMEM_BUDGET = 56 * 1024 * 1024

SSM_CHUNK = 128
ROW_TILE = 512
ATTN_TILE = 512
ATTN_QUERY_TILE = 512


def _cparams(semantics, vmem_bytes):
    return pltpu.CompilerParams(dimension_semantics=semantics,
                                vmem_limit_bytes=int(min(max(vmem_bytes, 16 * 1024 * 1024), V7X_VMEM_BUDGET)))


def _row_tile(t, pref):
    tile = min(t, pref)
    assert t % tile == 0, (t, tile)
    return tile


def _sigmoid(x):
    return 1.0 / (1.0 + jnp.exp(-x))


def _split_bf16(x, parts):
    out = []
    r = x
    for _ in range(parts):
        p = r.astype(BF16)
        out.append(p)
        r = r - p.astype(F32)
    return out


def _post_norm(x, out, gate, g, b, alpha):
    y = alpha * x + (1.0 + gate) * out
    mu = jnp.mean(y, axis=-1, keepdims=True)
    d = y - mu
    var = jnp.mean(d * d, axis=-1, keepdims=True)
    return d * lax.rsqrt(var + LN_EPS) * g + b


def _mod_kernel(c_ref, w_ref, b_ref, o_ref):
    c = c_ref[...]
    s = (c * _sigmoid(c)).astype(BF16)
    o_ref[0] = jnp.dot(s, w_ref[0].astype(BF16), preferred_element_type=F32) + b_ref[0]


def _mod_call(c_all, w_mod, b_mod):
    depth, d, d6 = w_mod.shape
    r = c_all.shape[0]
    tn = _row_tile(d6, 1536)
    return pl.pallas_call(
        _mod_kernel,
        grid=(depth, d6 // tn),
        in_specs=[pl.BlockSpec((r, d), lambda l, j: (0, 0)),
                  pl.BlockSpec((1, d, tn), lambda l, j: (l, 0, j)),
                  pl.BlockSpec((1, 1, tn), lambda l, j: (l, 0, j))],
        out_specs=pl.BlockSpec((1, r, tn), lambda l, j: (l, 0, j)),
        out_shape=jax.ShapeDtypeStruct((depth, r, d6), F32),
        name="adaln_mod",
        compiler_params=_cparams(("parallel", "parallel"), 3 * d * tn * 4),
    )(c_all, w_mod, b_mod.reshape(depth, 1, d6))


def _discretise(lr, li, ldt):
    dt = jnp.exp(ldt)
    mag = jnp.exp(lr * dt)
    a_re = mag * jnp.cos(li * dt)
    a_im = mag * jnp.sin(li * dt)
    den = lr * lr + li * li
    nr = a_re - 1.0
    k_re = (nr * lr + a_im * li) / den
    k_im = (a_im * lr - nr * li) / den
    return a_re, a_im, k_re, k_im


def _cmul(ar, ai, br, bi):
    return ar * br - ai * bi, ar * bi + ai * br


def _ssm_prep_kernel(lr_row, li_row, dt_row, lr_col, li_col, dt_col, b_re, b_im,
                     a_re_o, a_im_o, pneg_re_o, pneg_im_o, ppos_re_o, ppos_im_o, bb_re_o, bb_im_o):
    a_re, a_im, _, _ = _discretise(lr_row[...], li_row[...], dt_row[...])
    a_re_o[...] = a_re
    a_im_o[...] = a_im
    _, _, k_re, k_im = _discretise(lr_col[...], li_col[...], dt_col[...])
    br = b_re[...]
    bi = b_im[...]
    bb_re_o[...] = k_re * br - k_im * bi
    bb_im_o[...] = k_re * bi + k_im * br

    chunk = ppos_re_o.shape[0]
    t = lax.broadcasted_iota(jnp.int32, ppos_re_o.shape, 0)
    n2 = a_re * a_re + a_im * a_im
    for (sq_re, sq_im, o_re, o_im) in ((a_re, a_im, ppos_re_o, ppos_im_o),
                                       (a_re / n2, -a_im / n2, pneg_re_o, pneg_im_o)):
        p_re = jnp.ones(ppos_re_o.shape, F32)
        p_im = jnp.zeros(ppos_re_o.shape, F32)
        bit = 1
        while bit < chunk:
            on = (t & bit) != 0
            f_re = jnp.where(on, sq_re, 1.0)
            f_im = jnp.where(on, sq_im, 0.0)
            p_re, p_im = _cmul(p_re, p_im, f_re, f_im)
            sq_re, sq_im = _cmul(sq_re, sq_im, sq_re, sq_im)
            bit *= 2
        o_re[...] = p_re
        o_im[...] = p_im


def _ssm_prep(lam_re, lam_im, log_dt, b_re, b_im, c_re, c_im):
    g, p = lam_re.shape
    gs = b_re.shape[-1]
    gp = g * p
    dt = jnp.broadcast_to(log_dt[:, None], (g, p))
    rows = [a.reshape(1, gp) for a in (lam_re, lam_im, dt)]
    cols = [a.reshape(gp, 1) for a in (lam_re, lam_im, dt)]
    full = lambda shape: pl.BlockSpec(shape, lambda: tuple(0 for _ in shape))
    out_shapes = ([jax.ShapeDtypeStruct((1, gp), F32)] * 2 + [jax.ShapeDtypeStruct((SSM_CHUNK, gp), F32)] * 4
                  + [jax.ShapeDtypeStruct((gp, gs), F32)] * 2)
    a_re, a_im, pneg_re, pneg_im, ppos_re, ppos_im, bb_re, bb_im = pl.pallas_call(
        _ssm_prep_kernel,
        in_specs=[full((1, gp))] * 3 + [full((gp, 1))] * 3 + [full((gp, gs))] * 2,
        out_specs=[full(s.shape) for s in out_shapes],
        out_shape=out_shapes,
        name="ssm_prep",
        compiler_params=_cparams(None, 48 * SSM_CHUNK * gp * 4),
    )(*rows, *cols, b_re.reshape(gp, gs), b_im.reshape(gp, gs))

    gpb = V7X_MXU_DIM // gs
    nb = g // gpb
    eye = jnp.eye(gpb, dtype=F32)

    def in_proj(bb):
        t = bb.reshape(nb, gpb, p, gs).transpose(0, 1, 3, 2)
        return jnp.einsum("nghp,gk->nghkp", t, eye).reshape(nb, gpb * gs, gpb * p).astype(BF16)

    def out_proj(c):
        t = c.reshape(nb, gpb, gs, p)
        return jnp.einsum("nghp,gk->nkpgh", t, eye).reshape(nb, gpb * p, gpb * gs).astype(BF16)

    return dict(a_re=a_re, a_im=a_im, pneg_re=pneg_re, pneg_im=pneg_im, ppos_re=ppos_re, ppos_im=ppos_im,
                bb_re=in_proj(bb_re), bb_im=in_proj(bb_im), c_re=out_proj(c_re), c_im=out_proj(-c_im))


def _gelu_tanh(y):
    return 0.5 * y * (1.0 + jnp.tanh(0.7978845608028654 * (y + 0.044715 * (y * y * y))))


def _ssm_seq_kernel(x_ref, mod_ref, h0r_ref, h0i_ref, ar_ref, ai_ref, pnr_ref, pni_ref, ppr_ref, ppi_ref,
                    bbr_ref, bbi_ref, cr_ref, ci_ref, d_ref, tri_ref, z_ref, hr_ref, hi_ref):
    step = pl.program_id(1)

    @pl.when(step == 0)
    def _():
        hr_ref[0] = h0r_ref[0]
        hi_ref[0] = h0i_ref[0]

    chunk = x_ref.shape[1]
    u = x_ref[0] * (1.0 + mod_ref[0, 1]) + mod_ref[0, 0]
    ub = u.astype(BF16)
    nb, kin, kst = bbr_ref.shape
    sub = V7X_SUBLANES
    row0 = lax.broadcasted_iota(jnp.int32, (sub, kst), 0) == 0
    tri = tri_ref[...]
    cols = [slice(n * kst, (n + 1) * kst) for n in range(nb)]
    carry = [_cmul(ar_ref[:, cs], ai_ref[:, cs], hr_ref[0, :, cs], hi_ref[0, :, cs]) for cs in cols]
    xs = [(jnp.dot(ub[:, n * kin:(n + 1) * kin], bbr_ref[n], preferred_element_type=F32),
           jnp.dot(ub[:, n * kin:(n + 1) * kin], bbi_ref[n], preferred_element_type=F32)) for n in range(nb)]
    cums = []
    for cs, (x_re, x_im), (c_re, c_im) in zip(cols, xs, carry):
        x_re = jnp.concatenate([x_re[:sub] + jnp.where(row0, c_re, 0.0), x_re[sub:]], axis=0)
        x_im = jnp.concatenate([x_im[:sub] + jnp.where(row0, c_im, 0.0), x_im[sub:]], axis=0)
        scaled = _cmul(x_re, x_im, pnr_ref[:, cs], pni_ref[:, cs])
        cums.append([jnp.dot(tri, jnp.concatenate(_split_bf16(s, 2), axis=0), preferred_element_type=F32)
                     for s in scaled])
    ys = []
    for n, (cs, cum) in enumerate(zip(cols, cums)):
        h_re, h_im = _cmul(cum[0], cum[1], ppr_ref[:, cs], ppi_ref[:, cs])
        hr_ref[0, :, cs] = h_re[chunk - 1:chunk]
        hi_ref[0, :, cs] = h_im[chunk - 1:chunk]
        ys.append(jnp.dot(h_re.astype(BF16), cr_ref[n], preferred_element_type=F32)
                  + jnp.dot(h_im.astype(BF16), ci_ref[n], preferred_element_type=F32))
    y = jnp.concatenate(ys, axis=1) + u * d_ref[...]
    z_ref[0] = _gelu_tanh(y).astype(BF16)


def _ssm_seq_call(x, mod, h0_re, h0_im, sp, d_skip):
    n, t, d = x.shape
    gp = sp["a_re"].shape[1]
    chunk = SSM_CHUNK
    assert t % chunk == 0
    tri = jnp.tril(jnp.ones((chunk, chunk), F32)).astype(BF16)
    tri2 = jnp.concatenate([tri, tri], axis=1)
    const = lambda a: pl.BlockSpec(a.shape, lambda i, j: tuple(0 for _ in a.shape))
    consts = [sp["a_re"], sp["a_im"], sp["pneg_re"], sp["pneg_im"], sp["ppos_re"], sp["ppos_im"],
              sp["bb_re"], sp["bb_im"], sp["c_re"], sp["c_im"], d_skip.reshape(1, d), tri2]
    state = pl.BlockSpec((1, 1, gp), lambda i, j: (i, 0, 0))
    const_bytes = sum(a.size * a.dtype.itemsize for a in consts)
    z, h_re, h_im = pl.pallas_call(
        _ssm_seq_kernel,
        grid=(n, t // chunk),
        in_specs=[pl.BlockSpec((1, chunk, d), lambda i, j: (i, j, 0)),
                  pl.BlockSpec((1, 6, 1, d), lambda i, j: (i, 0, 0, 0)), state, state]
                 + [const(a) for a in consts],
        out_specs=[pl.BlockSpec((1, chunk, d), lambda i, j: (i, j, 0)), state, state],
        out_shape=[jax.ShapeDtypeStruct((n, t, d), BF16),
                   jax.ShapeDtypeStruct((n, 1, gp), F32), jax.ShapeDtypeStruct((n, 1, gp), F32)],
        name="ssm_seq",
        compiler_params=_cparams(("parallel", "arbitrary"), 2 * const_bytes + 40 * chunk * gp),
    )(x, mod, h0_re.reshape(n, 1, gp), h0_im.reshape(n, 1, gp), *consts)
    return z, h_re.reshape(n, gp), h_im.reshape(n, gp)


def _ssm_step_kernel(x_ref, mod_ref, h0r_ref, h0i_ref, ar_ref, ai_ref, bbr_ref, bbi_ref, cr_ref, ci_ref, d_ref,
                     z_ref, hr_ref, hi_ref):
    u = x_ref[0] * (1.0 + mod_ref[0, 1]) + mod_ref[0, 0]
    ub = u.astype(BF16)
    nb, kin, kst = bbr_ref.shape
    ys = []
    for n in range(nb):
        cs = slice(n * kst, (n + 1) * kst)
        ubn = ub[:, n * kin:(n + 1) * kin]
        c_re, c_im = _cmul(ar_ref[:, cs], ai_ref[:, cs], h0r_ref[:, cs], h0i_ref[:, cs])
        h_re = jnp.dot(ubn, bbr_ref[n], preferred_element_type=F32) + c_re
        h_im = jnp.dot(ubn, bbi_ref[n], preferred_element_type=F32) + c_im
        hr_ref[:, cs] = h_re
        hi_ref[:, cs] = h_im
        ys.append(jnp.dot(h_re.astype(BF16), cr_ref[n], preferred_element_type=F32)
                  + jnp.dot(h_im.astype(BF16), ci_ref[n], preferred_element_type=F32))
    y = jnp.concatenate(ys, axis=1) + u * d_ref[...]
    z_ref[0] = _gelu_tanh(y).astype(BF16)


def _ssm_step_call(x, mod, h0_re, h0_im, sp, d_skip):
    _, rows, d = x.shape
    gp = sp["a_re"].shape[1]
    args = [x, mod, h0_re, h0_im, sp["a_re"], sp["a_im"], sp["bb_re"], sp["bb_im"], sp["c_re"], sp["c_im"],
            d_skip.reshape(1, d)]
    full = lambda a: pl.BlockSpec(a.shape, lambda: tuple(0 for _ in a.shape))
    out_shapes = [jax.ShapeDtypeStruct((1, rows, d), BF16),
                  jax.ShapeDtypeStruct((rows, gp), F32), jax.ShapeDtypeStruct((rows, gp), F32)]
    return pl.pallas_call(
        _ssm_step_kernel,
        in_specs=[full(a) for a in args],
        out_specs=[full(s) for s in out_shapes],
        out_shape=out_shapes,
        name="ssm_step",
        compiler_params=_cparams(None, 4 * sum(a.size * a.dtype.itemsize for a in args)),
    )(*args)


def _mixer_post_kernel(a_ref, x_ref, mod_ref, w_ref, lng_ref, lnb_ref, *rest, glu, router, alpha):
    if router:
        wr_ref, br_ref, x1_ref, h2_ref, lg_ref = rest
    else:
        x1_ref, h2_ref = rest
    proj = jnp.dot(a_ref[0], w_ref[...], preferred_element_type=F32)
    if glu:
        d = proj.shape[1] // 2
        proj = proj[:, :d] * _sigmoid(proj[:, d:])
    x1 = _post_norm(x_ref[0], proj, mod_ref[0, 2], lng_ref[...], lnb_ref[...], alpha)
    x1_ref[0] = x1
    h2 = x1 * (1.0 + mod_ref[0, 4]) + mod_ref[0, 3]
    h2_ref[0] = h2.astype(BF16)
    if router:
        nt = (((1,), (1,)), ((), ()))
        h_hi, h_lo = _split_bf16(h2, 2)
        w_hi, w_lo = _split_bf16(wr_ref[...], 2)
        lg = (lax.dot_general(w_hi, h_hi, nt, preferred_element_type=F32)
              + lax.dot_general(w_lo, h_hi, nt, preferred_element_type=F32)
              + lax.dot_general(w_hi, h_lo, nt, preferred_element_type=F32))
        lg_ref[...] = lg + br_ref[...]


def _mixer_post_call(a, x, mod, w, ln_g, ln_b, alpha, *, glu, w_router=None, b_router=None):
    n, t, d = x.shape
    tm = _row_tile(t, ROW_TILE)
    rmod = mod.shape[2]
    router = w_router is not None
    row = lambda width: pl.BlockSpec((1, tm, width), lambda i, j: (i, j, 0))
    const = lambda arr: pl.BlockSpec(arr.shape, lambda i, j: tuple(0 for _ in arr.shape))
    mod_spec = pl.BlockSpec((1, 6, rmod, d), (lambda i, j: (i, 0, 0, 0)) if rmod == 1 else (lambda i, j: (i, 0, j, 0)))
    args = [a, x, mod, w, ln_g.reshape(1, d), ln_b.reshape(1, d)]
    in_specs = [row(a.shape[2]), row(d), mod_spec, const(w), const(args[4]), const(args[5])]
    out_specs = [row(d), row(d)]
    out_shape = [jax.ShapeDtypeStruct((n, t, d), F32), jax.ShapeDtypeStruct((n, t, d), BF16)]
    if router:
        e = w_router.shape[1]
        args += [w_router.T, b_router.reshape(e, 1)]
        in_specs += [const(args[-2]), const(args[-1])]
        out_specs.append(pl.BlockSpec((e, tm), lambda i, j: (0, i * (t // tm) + j)))
        out_shape.append(jax.ShapeDtypeStruct((e, n * t), F32))
    vmem = 2 * w.size * 2 + 12 * tm * w.shape[1] * 4 + 8 * tm * d * 4
    return pl.pallas_call(
        functools.partial(_mixer_post_kernel, glu=glu, router=router, alpha=alpha),
        grid=(n, t // tm), in_specs=in_specs, out_specs=out_specs, out_shape=out_shape,
        name="mixer_post",
        compiler_params=_cparams(("parallel", "parallel"), vmem),
    )(*args)


def _swiglu_chunks(h, wg_ref, wu_ref, wd_ref, lead, fc):
    f = wg_ref.shape[-1]
    acc = None
    for lo in range(0, f, fc):
        hi = min(lo + fc, f)
        g = jnp.dot(h, wg_ref[lead + (slice(None), slice(lo, hi))], preferred_element_type=F32)
        u = jnp.dot(h, wu_ref[lead + (slice(None), slice(lo, hi))], preferred_element_type=F32)
        a = (g * _sigmoid(g) * u).astype(BF16)
        part = jnp.dot(a, wd_ref[lead + (slice(lo, hi), slice(None))], preferred_element_type=F32)
        acc = part if acc is None else acc + part
    return acc


def _ffn_post_kernel(h_ref, x_ref, mod_ref, wg_ref, wu_ref, wd_ref, lng_ref, lnb_ref, o_ref, *, alpha, fc):
    out = _swiglu_chunks(h_ref[0], wg_ref, wu_ref, wd_ref, (), fc)
    o_ref[0] = _post_norm(x_ref[0], out, mod_ref[0, 5], lng_ref[...], lnb_ref[...], alpha)


def _ffn_post_call(h, x, mod, wg, wu, wd, ln_g, ln_b, alpha):
    n, t, d = x.shape
    f = wg.shape[1]
    tm = _row_tile(t, ROW_TILE)
    rmod = mod.shape[2]
    row = lambda: pl.BlockSpec((1, tm, d), lambda i, j: (i, j, 0))
    const = lambda arr: pl.BlockSpec(arr.shape, lambda i, j: tuple(0 for _ in arr.shape),
                                     pipeline_mode=pl.Buffered(1))
    mod_spec = pl.BlockSpec((1, 6, rmod, d), (lambda i, j: (i, 0, 0, 0)) if rmod == 1 else (lambda i, j: (i, 0, j, 0)))
    fc = 2 * V7X_MXU_DIM
    args = [h, x, mod, wg, wu, wd, ln_g.reshape(1, d), ln_b.reshape(1, d)]
    vmem = 3 * d * f * 2 + 10 * tm * d * 4 + 6 * tm * fc * 4
    return pl.pallas_call(
        functools.partial(_ffn_post_kernel, alpha=alpha, fc=fc),
        grid=(n, t // tm),
        in_specs=[row(), row(), mod_spec] + [const(a) for a in args[3:]],
        out_specs=row(),
        out_shape=jax.ShapeDtypeStruct((n, t, d), F32), name="ffn_post",
        compiler_params=_cparams(("parallel", "parallel"), vmem),
    )(*args)


MOE_DISPATCH_TILE = 512
MOE_ROW_TILE = 512
MOE_COMBINE_TILE = 128
BF16_ROWS = 16


def _route_kernel(lg_ref, upper_ref, gate_ref, sel_ref, rank_ref, cnt_ref):
    @pl.when(pl.program_id(0) == 0)
    def _():
        cnt_ref[...] = jnp.zeros_like(cnt_ref)

    lg = lg_ref[...]
    e = lg.shape[0]
    idx = lax.broadcasted_iota(jnp.int32, lg.shape, 0)
    m1 = jnp.max(lg, axis=0, keepdims=True)
    i1 = jnp.min(jnp.where(lg == m1, idx, e), axis=0, keepdims=True)
    rest = jnp.where(idx == i1, -jnp.inf, lg)
    m2 = jnp.max(rest, axis=0, keepdims=True)
    i2 = jnp.min(jnp.where(rest == m2, idx, e), axis=0, keepdims=True)
    e2 = jnp.exp(m2 - m1)
    den = 1.0 + e2
    gate_ref[...] = jnp.where(idx == i1, 1.0 / den, 0.0) + jnp.where(idx == i2, e2 / den, 0.0)
    sel = jnp.where(idx == i1, 1.0, 0.0) + jnp.where(idx == i2, 1.0, 0.0)
    sel_ref[...] = sel
    rank_ref[...] = cnt_ref[...] + jnp.dot(sel.astype(BF16), upper_ref[...], preferred_element_type=F32)
    cnt_ref[...] += jnp.sum(sel, axis=1, keepdims=True)


def _route_call(logits_t):
    e, m = logits_t.shape
    tr = _row_tile(m, ROW_TILE)
    upper = jnp.triu(jnp.ones((tr, tr), F32), k=1).astype(BF16)
    blk = pl.BlockSpec((e, tr), lambda i: (0, i))
    out_shape = [jax.ShapeDtypeStruct((e, m), F32)] * 3 + [jax.ShapeDtypeStruct((e, 1), F32)]
    return pl.pallas_call(
        _route_kernel, grid=(m // tr,),
        in_specs=[blk, pl.BlockSpec((tr, tr), lambda i: (0, 0))],
        out_specs=[blk, blk, blk, pl.BlockSpec((e, 1), lambda i: (0, 0))],
        out_shape=out_shape, name="moe_route",
        compiler_params=_cparams(("arbitrary",), 4 * tr * tr * 2 + 64 * e * tr * 4),
    )(logits_t, upper)


def _moe_schedule(gate_t, sel_t, rank_t, cnt, tt, tme, tc):
    e, m = sel_t.shape
    cnt_i = cnt[:, 0].astype(jnp.int32)
    seg_tiles = cnt_i // tme + 1
    seg_end = jnp.cumsum(seg_tiles)
    seg_start = seg_end - seg_tiles
    offsets = seg_start * tme
    n_tiles = (TOP_K * m) // tme + e
    tile_ids = jnp.arange(n_tiles, dtype=jnp.int32)
    tile_expert = jnp.minimum(jnp.sum(tile_ids[:, None] >= seg_end[None, :], axis=1), e - 1).astype(jnp.int32)
    tile_valid = (tile_ids < seg_end[-1]).astype(jnp.int32)
    rank_i = rank_t.astype(jnp.int32)
    rank_at = jnp.concatenate([rank_i[:, ::tt], cnt_i[:, None]], axis=1)
    done_tiles = rank_at // tme
    w0 = (rank_at - done_tiles * tme) // BF16_ROWS * BF16_ROWS
    completes = jnp.concatenate([rank_at[:, 1:] >= (done_tiles[:, :-1] + 1) * tme,
                                 jnp.ones((e, 1), bool)], axis=1).astype(jnp.int32)
    fits = rank_at[:, 1:] - rank_at[:, :-1] + BF16_ROWS <= _dispatch_narrow_window(tt)
    completes = completes + 2 * jnp.concatenate([fits, jnp.zeros((e, 1), bool)], axis=1).astype(jnp.int32)
    pos = jnp.where(sel_t > 0, offsets[:, None].astype(F32) + rank_t, -1.0).T
    blk0 = ((offsets[:, None] + rank_i[:, ::tc]) // tc).T
    return dict(base=(done_tiles * tme + w0).reshape(-1), w0=w0.reshape(-1), completes=completes.reshape(-1),
                out_tile=(seg_start[:, None] + done_tiles).reshape(-1), tile_expert=tile_expert,
                tile_valid=tile_valid, n_tiles=n_tiles, blk0=blk0.reshape(-1).astype(jnp.int32), pos=pos,
                gates=gate_t.T)


def _dispatch_narrow_window(tt):
    half = tt // 2
    return half if half % BF16_ROWS == 0 and half > BF16_ROWS else tt + BF16_ROWS


def _dispatch_kernel(base_ref, w0_ref, done_ref, tile_ref, h_ref, sel_ref, rank_ref, o_ref, stage_ref, *, tme):
    del tile_ref
    ex = pl.program_id(0)
    j = pl.program_id(1)
    steps = pl.num_programs(1)
    flat = ex * steps + j
    tt = h_ref.shape[0]
    win = tt + BF16_ROWS

    @pl.when(j == 0)
    def _():
        stage_ref[...] = jnp.zeros_like(stage_ref)

    def place(rows_in_window):
        w0 = pl.multiple_of(w0_ref[flat], BF16_ROWS)
        local = rank_ref[pl.ds(ex, 1), :] - base_ref[flat].astype(F32)
        local = jnp.where(sel_ref[pl.ds(ex, 1), :] > 0, local, -1.0)
        rows = lax.broadcasted_iota(jnp.int32, (rows_in_window, tt), 0).astype(F32)
        onehot = jnp.where(rows == local, 1.0, 0.0).astype(BF16)
        stage_ref[pl.ds(w0, rows_in_window), :] += jnp.dot(onehot, h_ref[...], preferred_element_type=F32)

    narrow = _dispatch_narrow_window(tt)
    flag = done_ref[flat]

    @pl.when((j < steps - 1) & (flag >= 2))
    def _():
        place(narrow)

    @pl.when((j < steps - 1) & (flag < 2))
    def _():
        place(win)

    @pl.when(flag % 2 == 1)
    def _():
        o_ref[...] = stage_ref[0:tme, :].astype(BF16)
        tail = stage_ref[tme:, :]
        stage_ref[...] = jnp.zeros_like(stage_ref)
        stage_ref[0:win, :] = tail


def _dispatch_call(h, sel_t, rank_t, sched, tt, tme):
    m, d = h.shape
    e = sel_t.shape[0]
    nj = m // tt
    steps = nj + 1
    clamp = lambda j: jnp.minimum(j, nj - 1)
    grid_spec = pltpu.PrefetchScalarGridSpec(
        num_scalar_prefetch=4, grid=(e, steps),
        in_specs=[pl.BlockSpec((tt, d), lambda ex, j, *_: (clamp(j), 0)),
                  pl.BlockSpec((e, tt), lambda ex, j, *_: (0, clamp(j))),
                  pl.BlockSpec((e, tt), lambda ex, j, *_: (0, clamp(j)))],
        out_specs=pl.BlockSpec((tme, d), lambda ex, j, base, w0, done, tile: (tile[ex * steps + j], 0)),
        scratch_shapes=[pltpu.VMEM((tme + tt + BF16_ROWS, d), F32)])
    return pl.pallas_call(
        functools.partial(_dispatch_kernel, tme=tme), grid_spec=grid_spec,
        out_shape=jax.ShapeDtypeStruct((sched["n_tiles"] * tme, d), BF16), name="moe_dispatch",
        compiler_params=_cparams(("arbitrary", "arbitrary"), 12 * (tme + tt) * d * 4),
    )(sched["base"], sched["w0"], sched["completes"], sched["out_tile"], h, sel_t, rank_t)


def _group_ffn_kernel(texp_ref, valid_ref, x_ref, wg_ref, wu_ref, wd_ref, y_ref, *, fc):
    del texp_ref
    live = valid_ref[pl.program_id(0)] == 1

    @pl.when(live)
    def _():
        y_ref[...] = _swiglu_chunks(x_ref[...], wg_ref, wu_ref, wd_ref, (0, 0), fc).astype(BF16)

    @pl.when(jnp.logical_not(live))
    def _():
        y_ref[...] = jnp.zeros_like(y_ref)


def _group_ffn_call(xs, sched, wg, wu, wd, layer, tme):
    rows, d = xs.shape
    f = wg.shape[3]
    fc = 2 * V7X_MXU_DIM
    row = pl.BlockSpec((tme, d), lambda i, texp, valid: (i, 0))
    grid_spec = pltpu.PrefetchScalarGridSpec(
        num_scalar_prefetch=2, grid=(rows // tme,),
        in_specs=[row,
                  pl.BlockSpec((1, 1, d, f), lambda i, texp, valid: (layer, texp[i], 0, 0)),
                  pl.BlockSpec((1, 1, d, f), lambda i, texp, valid: (layer, texp[i], 0, 0)),
                  pl.BlockSpec((1, 1, f, d), lambda i, texp, valid: (layer, texp[i], 0, 0))],
        out_specs=row)
    return pl.pallas_call(
        functools.partial(_group_ffn_kernel, fc=fc), grid_spec=grid_spec,
        out_shape=jax.ShapeDtypeStruct((rows, d), BF16), name="moe_group_ffn",
        compiler_params=_cparams(("arbitrary",), 2 * 3 * d * f * 2 + 8 * tme * d * 4 + 6 * tme * fc * 4),
    )(sched["tile_expert"], sched["tile_valid"], xs, wg, wu, wd)


def _combine_post_kernel(blk_ref, x_ref, mod_ref, pos_ref, gate_ref, lng_ref, lnb_ref, *rest, experts, alpha):
    y_refs, o_ref = rest[:2 * experts], rest[2 * experts]
    tc = x_ref.shape[1]
    tile = pl.program_id(0) * pl.num_programs(1) + pl.program_id(1)
    lane = lax.broadcasted_iota(jnp.int32, (tc, 2 * tc), 1).astype(F32)
    pos = pos_ref[...]
    gates = gate_ref[...]
    acc = jnp.zeros(x_ref.shape[1:], F32)
    for e in range(experts):
        base = (blk_ref[tile * experts + e] * tc).astype(F32)
        onehot = jnp.where(lane == pos[:, e:e + 1] - base, 1.0, 0.0).astype(BF16)
        window = jnp.concatenate([y_refs[2 * e][...], y_refs[2 * e + 1][...]], axis=0)
        acc = acc + gates[:, e:e + 1] * jnp.dot(onehot, window, preferred_element_type=F32)
    o_ref[0] = _post_norm(x_ref[0], acc, mod_ref[0, 5], lng_ref[...], lnb_ref[...], alpha)


def _combine_post_call(ys, x, mod, sched, ln_g, ln_b, alpha, tc):
    n, t, d = x.shape
    e = sched["pos"].shape[1]
    nj = t // tc
    nblk = ys.shape[0] // tc
    rmod = mod.shape[2]
    row = pl.BlockSpec((1, tc, d), lambda i, j, blk: (i, j, 0))
    per_tok = pl.BlockSpec((tc, e), lambda i, j, blk: (i * nj + j, 0))
    const = lambda arr: pl.BlockSpec(arr.shape, lambda i, j, blk: tuple(0 for _ in arr.shape))
    mod_spec = pl.BlockSpec((1, 6, rmod, d),
                            (lambda i, j, blk: (i, 0, 0, 0)) if rmod == 1 else (lambda i, j, blk: (i, 0, j, 0)))

    def gather_spec(ex, half):
        return pl.BlockSpec((tc, d), lambda i, j, blk: (jnp.minimum(blk[(i * nj + j) * e + ex] + half, nblk - 1), 0))

    lng, lnb = ln_g.reshape(1, d), ln_b.reshape(1, d)
    grid_spec = pltpu.PrefetchScalarGridSpec(
        num_scalar_prefetch=1, grid=(n, nj),
        in_specs=[row, mod_spec, per_tok, per_tok, const(lng), const(lnb)]
                 + [gather_spec(ex, half) for ex in range(e) for half in range(2)],
        out_specs=row)
    return pl.pallas_call(
        functools.partial(_combine_post_kernel, experts=e, alpha=alpha), grid_spec=grid_spec,
        out_shape=jax.ShapeDtypeStruct((n, t, d), F32), name="moe_combine_post",
        compiler_params=_cparams(("parallel", "parallel"), 4 * 2 * e * tc * d * 2 + 16 * tc * d * 4),
    )(sched["blk0"], x, mod, sched["pos"], sched["gates"], lng, lnb, *([ys] * (2 * e)))


def _moe_call(h2, logits_t, x1, mod, wg, wu, wd, layer, ln_g, ln_b, alpha):
    n, t, d = x1.shape
    m = n * t
    tt = min(MOE_DISPATCH_TILE, m)
    tme = min(MOE_ROW_TILE, m)
    tc = min(MOE_COMBINE_TILE, t)
    gate_t, sel_t, rank_t, cnt = _route_call(logits_t)
    sched = _moe_schedule(gate_t, sel_t, rank_t, cnt, tt, tme, tc)
    xs = _dispatch_call(h2.reshape(m, d), sel_t, rank_t, sched, tt, tme)
    ys = _group_ffn_call(xs, sched, wg, wu, wd, layer, tme)
    return _combine_post_call(ys, x1, mod, sched, ln_g, ln_b, alpha, tc)


def _log_sigmoid(z):
    return -(jnp.maximum(-z, 0.0) + jnp.log1p(jnp.exp(-jnp.abs(z))))


ATTN_LANES = V7X_LANES
AUG_TERMS = 3
LOG2E = 1.4426950408889634


def _aug_selectors(heads, hd, offset):
    rows = jnp.arange(AUG_TERMS * heads)
    col = (rows % heads) * ATTN_LANES + hd + offset + rows // heads
    return jnp.zeros((AUG_TERMS * heads, heads * ATTN_LANES), F32).at[rows, col].set(1.0).astype(BF16)


def _aug_ones(heads, hd, offset, count):
    lane = jnp.arange(heads * ATTN_LANES) % ATTN_LANES
    return ((lane >= hd + offset) & (lane < hd + offset + count)).astype(F32).reshape(1, heads * ATTN_LANES)


def _place_terms(x, sel_ref, sign):
    terms = jnp.concatenate([t.astype(F32) for t in _split_bf16(x, AUG_TERMS)], axis=1).astype(BF16)
    return sign * jnp.dot(terms, sel_ref[...], preferred_element_type=F32)


def _store_head_operands(dense, tail, out_ref, hd):
    per_tile = ATTN_LANES // hd
    own = lax.broadcasted_iota(jnp.int32, (dense.shape[0], ATTN_LANES), 1) < hd
    for h in range(out_ref.shape[1]):
        tile = dense[:, (h // per_tile) * ATTN_LANES:(h // per_tile + 1) * ATTN_LANES]
        if h % per_tile:
            tile = pltpu.roll(tile, ATTN_LANES - (h % per_tile) * hd, axis=1)
        out_ref[0, h] = jnp.where(own, tile, tail[:, h * ATTN_LANES:(h + 1) * ATTN_LANES]).astype(BF16)


def _kv_kernel(x_ref, wk_ref, wv_ref, wf_ref, bf_ref, *rest, cumulative):
    if cumulative:
        tri_ref, sel_ref, kone_ref, vone_ref, k_ref, v_ref, lf_ref, fk_ref, ka_ref, va_ref, carry_ref = rest
    else:
        k_ref, v_ref, lf_ref = rest
    xb = x_ref[0].astype(BF16)
    k = jnp.dot(xb, wk_ref[...], preferred_element_type=F32)
    v = jnp.dot(xb, wv_ref[...], preferred_element_type=F32)
    k_ref[0] = k
    v_ref[0] = v
    lf = _log_sigmoid(jnp.dot(xb, wf_ref[...], preferred_element_type=F32) + bf_ref[...])
    lf_ref[0] = lf
    if cumulative:
        @pl.when(pl.program_id(1) == 0)
        def _():
            carry_ref[...] = jnp.zeros_like(carry_ref)

        fk = carry_ref[...] + jnp.dot(tri_ref[...], jnp.concatenate(_split_bf16(lf, 3), axis=0),
                                      preferred_element_type=F32)
        fk_ref[0] = fk
        carry_ref[...] = fk[fk.shape[0] - 1:]
        hd = k.shape[1] // ka_ref.shape[1]
        _store_head_operands(k, kone_ref[...] + _place_terms(fk * LOG2E, sel_ref, -1.0), ka_ref, hd)
        _store_head_operands(v, jnp.broadcast_to(vone_ref[...], (v.shape[0], vone_ref.shape[1])), va_ref, hd)


def _kv_call(x, wk, wv, wf, bf, *, cumulative):
    n, t, d = x.shape
    hh = wf.shape[1]
    hd = d // hh
    tm = _row_tile(t, ROW_TILE)
    row = lambda width: pl.BlockSpec((1, tm, width), lambda i, j: (i, j, 0))
    const = lambda arr: pl.BlockSpec(arr.shape, lambda i, j: tuple(0 for _ in arr.shape))
    args = [x, wk, wv, wf, bf.reshape(1, hh)]
    out_specs = [row(d), row(d), row(hh)]
    out_shape = [jax.ShapeDtypeStruct((n, t, d), F32)] * 2 + [jax.ShapeDtypeStruct((n, t, hh), F32)]
    scratch = []
    if cumulative:
        tri = jnp.tril(jnp.ones((tm, tm), F32)).astype(BF16)
        args += [jnp.concatenate([tri] * 3, axis=1),
                 _aug_selectors(hh, hd, AUG_TERMS), _aug_ones(hh, hd, 0, AUG_TERMS), _aug_ones(hh, hd, 0, 1)]
        head_rows = pl.BlockSpec((1, hh, tm, ATTN_LANES), lambda i, j: (i, 0, j, 0))
        out_specs += [row(hh), head_rows, head_rows]
        out_shape += [jax.ShapeDtypeStruct((n, t, hh), F32)] + [jax.ShapeDtypeStruct((n, hh, t, ATTN_LANES), BF16)] * 2
        scratch = [pltpu.VMEM((1, hh), F32)]
    return pl.pallas_call(
        functools.partial(_kv_kernel, cumulative=cumulative),
        grid=(n, t // tm),
        in_specs=[row(d)] + [const(a) for a in args[1:]],
        out_specs=out_specs, out_shape=out_shape, scratch_shapes=scratch, name="kv_proj",
        compiler_params=_cparams(("parallel", "arbitrary"), 16 * d * d + 40 * tm * d * 4),
    )(*args)


def _q_kernel(x_ref, mod_ref, w_ref, *rest, scale, augmented):
    h = (x_ref[0] * (1.0 + mod_ref[0, 1]) + mod_ref[0, 0]).astype(BF16)
    q = jnp.dot(h, w_ref[...], preferred_element_type=F32) * scale
    if augmented:
        fq_ref, sel_ref, one_ref, q_ref = rest
        _store_head_operands(q, one_ref[...] + _place_terms(fq_ref[0] * LOG2E, sel_ref, 1.0), q_ref,
                             q.shape[1] // q_ref.shape[1])
    else:
        q_ref, = rest
        q_ref[0] = q.astype(BF16)


def _q_call(x, mod, wq, heads, fq=None):
    n, t, d = x.shape
    hd = d // heads
    tm = _row_tile(t, ROW_TILE)
    rmod = mod.shape[2]
    row = lambda width: pl.BlockSpec((1, tm, width), lambda i, j: (i, j, 0))
    const = lambda arr: pl.BlockSpec(arr.shape, lambda i, j: tuple(0 for _ in arr.shape))
    mod_spec = pl.BlockSpec((1, 6, rmod, d), (lambda i, j: (i, 0, 0, 0)) if rmod == 1 else (lambda i, j: (i, 0, j, 0)))
    augmented = fq is not None
    if augmented:
        args = [x, mod, wq, fq, _aug_selectors(heads, hd, 0),
                _aug_ones(heads, hd, AUG_TERMS, AUG_TERMS)]
        in_specs = [row(d), mod_spec, const(args[2]), row(heads), const(args[4]), const(args[5])]
        out_specs = pl.BlockSpec((1, heads, tm, ATTN_LANES), lambda i, j: (i, 0, j, 0))
        out_shape = jax.ShapeDtypeStruct((n, heads, t, ATTN_LANES), BF16)
        scale = hd ** -0.5 * LOG2E
    else:
        args = [x, mod, wq]
        in_specs = [row(d), mod_spec, const(wq)]
        out_specs = row(d)
        out_shape = jax.ShapeDtypeStruct((n, t, d), BF16)
        scale = hd ** -0.5
    return pl.pallas_call(
        functools.partial(_q_kernel, scale=scale, augmented=augmented),
        grid=(n, t // tm), in_specs=in_specs, out_specs=out_specs, out_shape=out_shape, name="q_proj",
        compiler_params=_cparams(("parallel", "parallel"), 8 * d * d + 24 * tm * d * 4),
    )(*args)


ATTN_ROW_BLOCK = 256
ATTN_HEAD_GROUP = 16


def _attn_kernel(qt_ref, kt_ref, q_ref, k_ref, v_ref, o_ref, m_ref, acc_ref, *, hd, ratio):
    qi = qt_ref[pl.program_id(1)]
    kj = kt_ref[pl.program_id(1)]
    heads, tq = q_ref.shape[1], q_ref.shape[2]
    tk = k_ref.shape[2]
    lanes = acc_ref.shape[2]
    rb = min(ATTN_ROW_BLOCK, tq)
    group = math.gcd(heads, ATTN_HEAD_GROUP)
    nt = (((1,), (1,)), ((), ()))
    first_diag = qi * ratio

    @pl.when(kj == 0)
    def _():
        m_ref[...] = jnp.full_like(m_ref, NEG_INF)
        acc_ref[...] = jnp.zeros_like(acc_ref)

    def sweep(diagonal):
        blocks = [slice(r * rb, (r + 1) * rb) for r in range(tq // rb)]
        shift = (kj - first_diag) * tk

        def group_body(g, carry):
            hs = [g * group + i for i in range(group)]
            scores = [[lax.dot_general(q_ref[0, h, rows, :], k_ref[0, h], nt, preferred_element_type=F32)
                       for rows in blocks] for h in hs]
            for h, s_h in zip(hs, scores):
                vh = v_ref[0, h]
                m_all = m_ref[h]
                acc_all = acc_ref[h]
                m_out, acc_out = [], []
                for r, (rows, s) in enumerate(zip(blocks, s_h)):
                    if diagonal:
                        qpos = r * rb + lax.broadcasted_iota(jnp.int32, (rb, tk), 0)
                        kpos = shift + lax.broadcasted_iota(jnp.int32, (rb, tk), 1)
                        s = jnp.where(kpos <= qpos, s, NEG_INF)
                    m_prev = m_all[rows]
                    m_new = jnp.maximum(m_prev, jnp.max(s, axis=-1, keepdims=True))
                    p = jnp.exp2(s - jnp.concatenate([m_new] * (tk // lanes), axis=1))
                    acc_out.append(jnp.exp2(m_prev - m_new) * acc_all[rows]
                                   + jnp.dot(p.astype(BF16), vh, preferred_element_type=F32))
                    m_out.append(m_new)
                m_ref[h] = jnp.concatenate(m_out, axis=0)
                acc_ref[h] = jnp.concatenate(acc_out, axis=0)
            return carry
        lax.fori_loop(0, heads // group, group_body, 0)

    @pl.when(kj < first_diag)
    def _():
        sweep(False)

    @pl.when((kj >= first_diag) & (kj < first_diag + ratio))
    def _():
        sweep(True)

    @pl.when(kj == first_diag + ratio - 1)
    def _():
        for h in range(heads):
            a = acc_ref[h]
            o_ref[0, :, h * hd:(h + 1) * hd] = (a[:, :hd] / a[:, hd:hd + 1]).astype(BF16)


def _attn_call(qa, ka, va, hd):
    n, heads, t, lanes = qa.shape
    tk = _row_tile(t, ATTN_TILE)
    tq = _row_tile(t, ATTN_QUERY_TILE)
    ratio = tq // tk
    assert tq == ratio * tk and tk % lanes == 0
    pairs = [(a, b) for a in range(t // tq) for b in range((a + 1) * ratio)]
    q_tab = jnp.asarray([a for a, _ in pairs], jnp.int32)
    k_tab = jnp.asarray([b for _, b in pairs], jnp.int32)
    kv_spec = pl.BlockSpec((1, heads, tk, lanes), lambda i, s, qt, kt: (i, 0, kt[s], 0))
    group = math.gcd(heads, ATTN_HEAD_GROUP)
    grid_spec = pltpu.PrefetchScalarGridSpec(
        num_scalar_prefetch=2, grid=(n, len(pairs)),
        in_specs=[pl.BlockSpec((1, heads, tq, lanes), lambda i, s, qt, kt: (i, 0, qt[s], 0)), kv_spec, kv_spec],
        out_specs=pl.BlockSpec((1, tq, heads * hd), lambda i, s, qt, kt: (i, qt[s], 0)),
        scratch_shapes=[pltpu.VMEM((heads, tq, lanes), F32), pltpu.VMEM((heads, tq, lanes), F32)])
    return pl.pallas_call(
        functools.partial(_attn_kernel, hd=hd, ratio=ratio), grid_spec=grid_spec,
        out_shape=jax.ShapeDtypeStruct((n, t, heads * hd), BF16),
        name="causal_attn",
        compiler_params=_cparams(("parallel", "arbitrary"),
                                 heads * tq * lanes * (2 * 4 + 2 * 2) + 2 * 2 * heads * tk * lanes * 2
                                 + 2 * tq * heads * hd * 2 + 3 * group * tq * tk * 4),
    )(q_tab, k_tab, qa, ka, va)


def _column_to_row(col, eye):
    return jnp.sum(jnp.where(eye > 0, col, 0.0), axis=0, keepdims=True)


DECODE_PAGES_PER_STEP = 8


def _decode_attn_kernel(pt_ref, qb_ref, *rest, g):
    del pt_ref
    kt_refs, vt_refs, lft_refs = rest[:g], rest[g:2 * g], rest[2 * g:3 * g]
    (knew_ref, vnew_ref, lfnew_ref, later_ref, diag_ref, eye_ref, o_ref, m_ref, l_ref, acc_ref, suf_ref) = rest[3 * g:]
    step = pl.program_id(1)
    qb = qb_ref[0]
    eye = eye_ref[...]

    @pl.when(step == 0)
    def _():
        m_ref[...] = jnp.dot(qb, knew_ref[0].astype(BF16), preferred_element_type=F32)
        l_ref[...] = jnp.ones_like(l_ref)
        acc_ref[...] = jnp.broadcast_to(vnew_ref[0], acc_ref.shape)
        suf_ref[...] = lfnew_ref[0]

    scores = []
    after = suf_ref[...]
    for kt_ref, lft_ref in zip(kt_refs, lft_refs):
        lf = lft_ref[0]
        lf3 = jnp.concatenate(_split_bf16(lf, 3), axis=1)
        scores.append(jnp.dot(qb, kt_ref[0].astype(BF16), preferred_element_type=F32)
                      + jnp.dot(lf3, later_ref[...], preferred_element_type=F32) + after)
        after = after + jnp.sum(lf, axis=1, keepdims=True)
    suf_ref[...] = after
    m_prev = m_ref[...]
    m_new = m_prev
    for s in scores:
        m_new = jnp.maximum(m_new, jnp.max(s, axis=1, keepdims=True))
    alpha = jnp.exp(m_prev - m_new)
    l_new = alpha * l_ref[...]
    acc = _column_to_row(alpha, eye) * acc_ref[...]
    for s, vt_ref in zip(scores, vt_refs):
        p = jnp.exp(s - m_new)
        l_new = l_new + jnp.sum(p, axis=1, keepdims=True)
        acc = acc + lax.dot_general(vt_ref[0].astype(BF16), p.astype(BF16), (((1,), (1,)), ((), ())),
                                    preferred_element_type=F32)
    l_ref[...] = l_new
    acc_ref[...] = acc
    m_ref[...] = m_new

    @pl.when(step == pl.num_programs(1) - 1)
    def _():
        on_diag = diag_ref[...] > 0
        num = jnp.sum(jnp.where(on_diag, acc_ref[...], 0.0), axis=1, keepdims=True)
        den = jnp.sum(jnp.where(on_diag, _column_to_row(l_ref[...], eye), 0.0), axis=1, keepdims=True)
        o_ref[0] = (num / den).astype(BF16)


def _decode_attn_call(page_table, q, cache_k, cache_v, cache_logf, k_new, v_new, lf_new, heads):
    nseq, npages = page_table.shape
    n_phys, page, _, hd = cache_k.shape
    d = heads * hd
    seg = jnp.repeat(jnp.eye(heads, dtype=F32), hd, axis=1)
    qb = q[:, None, :] * seg[None].astype(BF16)
    later = jnp.tril(jnp.ones((page, page), F32), k=-1).astype(BF16)
    later3 = jnp.concatenate([later] * 3, axis=0)
    kt = cache_k.transpose(0, 2, 3, 1).reshape(n_phys, d, page)
    vt = cache_v.transpose(0, 2, 3, 1).reshape(n_phys, d, page)
    lft = cache_logf.transpose(0, 2, 1)
    g = math.gcd(npages, DECODE_PAGES_PER_STEP)

    def paged(rows):
        return [pl.BlockSpec((1, rows, page), functools.partial(
            lambda b, p, pt, i: (pt[b, npages - 1 - (p * g + i)], 0, 0), i=i)) for i in range(g)]

    per_seq = lambda arr: pl.BlockSpec((1,) + arr.shape[1:], lambda b, p, pt: (b,) + tuple(0 for _ in arr.shape[1:]))
    const = lambda arr: pl.BlockSpec(arr.shape, lambda b, p, pt: tuple(0 for _ in arr.shape))
    tail = [k_new.reshape(nseq, d, 1), v_new.reshape(nseq, d, 1), lf_new.reshape(nseq, heads, 1),
            later3, seg.T, jnp.eye(heads, dtype=F32)]
    args = [qb] + [kt] * g + [vt] * g + [lft] * g + tail
    grid_spec = pltpu.PrefetchScalarGridSpec(
        num_scalar_prefetch=1, grid=(nseq, npages // g),
        in_specs=[per_seq(qb)] + paged(d) + paged(d) + paged(heads)
                 + [per_seq(a) for a in tail[:3]] + [const(a) for a in tail[3:]],
        out_specs=pl.BlockSpec((1, d, 1), lambda b, p, pt: (b, 0, 0)),
        scratch_shapes=[pltpu.VMEM((heads, 1), F32), pltpu.VMEM((heads, 1), F32), pltpu.VMEM((d, heads), F32),
                        pltpu.VMEM((heads, 1), F32)])
    out = pl.pallas_call(
        functools.partial(_decode_attn_kernel, g=g), grid_spec=grid_spec,
        out_shape=jax.ShapeDtypeStruct((nseq, d, 1), BF16), name="paged_decode_attn",
        compiler_params=_cparams(("parallel", "arbitrary"), (8 + 7 * g) * page * d * 4),
    )(page_table, *args)
    return out.reshape(1, nseq, d)


def _trunk(x, mods, h0_re, h0_im, wts, ssm, *, sequence, paged=None):
    depth = mods.shape[0]
    n_a = wts["w_glu"].shape[0]
    heads = wts["w_f"].shape[1]
    nb, rows, d = x.shape
    hd = d // heads
    alpha = (2.0 * depth) ** 0.25
    new_re, new_im = [], []
    k_new = v_new = lf_new = None
    fk = k_aug = v_aug = None
    for l in range(depth):
        mod = mods[l]
        moe = l % 2 == 1
        li = l // 2
        router = dict(w_router=wts["w_router"][li], b_router=wts["b_router"][li]) if moe else {}
        if l < n_a:
            if sequence:
                z, hr, hi = _ssm_seq_call(x, mod, h0_re[l], h0_im[l], ssm[l], wts["ssm_d"][l])
            else:
                z, hr, hi = _ssm_step_call(x, mod, h0_re[l], h0_im[l], ssm[l], wts["ssm_d"][l])
            new_re.append(hr)
            new_im.append(hi)
            res = _mixer_post_call(z, x, mod, wts["w_glu"][l], wts["ln_g"][l, 0], wts["ln_b"][l, 0], alpha,
                                   glu=True, **router)
        else:
            lb = l - n_a
            if sequence:
                o = _attn_call(_q_call(x, mod, wts["w_q"][lb], heads, fq=fk), k_aug, v_aug, hd)
            else:
                q = _q_call(x, mod, wts["w_q"][lb], heads)
                o = _decode_attn_call(paged[0], q[0], paged[1], paged[2], paged[3], k_new[0], v_new[0], lf_new[0],
                                      heads)
            res = _mixer_post_call(o, x, mod, wts["w_o"][lb], wts["ln_g"][l, 0], wts["ln_b"][l, 0], alpha,
                                   glu=False, **router)
        if moe:
            x1, h2, logits_t = res
            x = _moe_call(h2, logits_t, x1, mod, wts["w_exp_gate"], wts["w_exp_up"], wts["w_exp_down"], li,
                          wts["ln_g"][l, 1], wts["ln_b"][l, 1], alpha)
        else:
            x1, h2 = res
            x = _ffn_post_call(h2, x1, mod, wts["w_ff_gate"][li], wts["w_ff_up"][li], wts["w_ff_down"][li],
                               wts["ln_g"][l, 1], wts["ln_b"][l, 1], alpha)
        if l == n_a - 1:
            outs = _kv_call(x, wts["w_k"], wts["w_v"], wts["w_f"], wts["b_f"], cumulative=sequence)
            k_new, v_new, lf_new = outs[:3]
            if sequence:
                fk, k_aug, v_aug = outs[3:]
    return x, jnp.stack(new_re), jnp.stack(new_im), k_new, v_new, lf_new


def kernel(x_prompt, x_sample, c_prompt, c_sample, cache_k, cache_v, cache_logf, state_ssm_re, state_ssm_im,
           page_table, w_mod, b_mod, ln_g, ln_b, ssm_lam_re, ssm_lam_im, ssm_log_dt, ssm_b_re, ssm_b_im,
           ssm_c_re, ssm_c_im, ssm_d, w_glu, w_k, w_v, w_f, b_f, w_q, w_o, w_ff_gate, w_ff_up, w_ff_down,
           w_router, b_router, w_exp_gate, w_exp_up, w_exp_down):
    batch, seq, d = x_prompt.shape
    dec_batch, dec_seq, _ = x_sample.shape
    assert dec_seq == 1
    depth = w_mod.shape[0]
    n_a, g, p = ssm_lam_re.shape
    heads = w_f.shape[1]
    hd = d // heads
    gp = g * p

    n_c = batch + dec_batch
    c_all = jnp.concatenate([c_prompt, c_sample], axis=0)
    c_all = jnp.pad(c_all, ((0, (-n_c) % 8), (0, 0)))
    mods = _mod_call(c_all, w_mod, b_mod)
    mods_p = mods[:, :batch].reshape(depth, batch, 6, 1, d)
    mods_s = mods[:, batch:n_c].reshape(depth, dec_batch, 6, d).transpose(0, 2, 1, 3)[:, None]

    bf = lambda w: w.astype(BF16)
    wts = dict(ln_g=ln_g, ln_b=ln_b, ssm_d=ssm_d, w_glu=bf(w_glu), w_k=bf(w_k), w_v=bf(w_v), w_f=bf(w_f), b_f=b_f,
               w_q=bf(w_q), w_o=bf(w_o), w_ff_gate=bf(w_ff_gate), w_ff_up=bf(w_ff_up), w_ff_down=bf(w_ff_down),
               w_router=w_router, b_router=b_router, w_exp_gate=bf(w_exp_gate), w_exp_up=bf(w_exp_up),
               w_exp_down=bf(w_exp_down))
    ssm = [_ssm_prep(ssm_lam_re[l], ssm_lam_im[l], ssm_log_dt[l], ssm_b_re[l], ssm_b_im[l], ssm_c_re[l],
                     ssm_c_im[l]) for l in range(n_a)]

    h0 = jnp.zeros((n_a, batch, gp), F32)
    y_p, re_p, im_p, k_p, v_p, lf_p = _trunk(x_prompt, mods_p, h0, h0, wts, ssm, sequence=True)

    x_s = x_sample.reshape(1, dec_batch, d)
    y_s, re_s, im_s, k_s, v_s, lf_s = _trunk(
        x_s, mods_s, state_ssm_re.reshape(n_a, dec_batch, gp), state_ssm_im.reshape(n_a, dec_batch, gp), wts, ssm,
        sequence=False, paged=(page_table, cache_k, cache_v, cache_logf))

    return (y_p, y_s.reshape(dec_batch, 1, d),
            re_p.reshape(n_a, batch, g, p), im_p.reshape(n_a, batch, g, p),
            k_p.reshape(batch, seq, heads, hd), v_p.reshape(batch, seq, heads, hd), lf_p,
            re_s.reshape(n_a, dec_batch, g, p), im_s.reshape(n_a, dec_batch, g, p),
            k_s.reshape(dec_batch, 1, heads, hd), v_s.reshape(dec_batch, 1, heads, hd),
            lf_s.reshape(dec_batch, 1, heads))
```

```python
import functools
import math

import jax
import jax.numpy as jnp
from jax import lax
from jax.experimental import pallas as pl
from jax.experimental.pallas import tpu as pltpu

F32 = jnp.float32
BF16 = jnp.bfloat16

GROUP_SIZE = 16
TOP_K = 2
LN_EPS = 1e-5
NEG_INF = -1e30

V7X_MXU_DIM = 256
V7X_LANES = 128
V7X_SUBLANES = 8
V7X_VMEM_BUDGET = 56 * 1024 * 1024

SSM_CHUNK = 128
ROW_TILE = 512
ATTN_TILE = 512
ATTN_QUERY_TILE = 512


def _cparams(semantics, vmem_bytes):
    return pltpu.CompilerParams(dimension_semantics=semantics,
                                vmem_limit_bytes=int(min(max(vmem_bytes, 16 * 1024 * 1024), V7X_VMEM_BUDGET)))


def _row_tile(t, pref):
    tile = min(t, pref)
    assert t % tile == 0, (t, tile)
    return tile


def _sigmoid(x):
    return 1.0 / (1.0 + jnp.exp(-x))


def _split_bf16(x, parts):
    out = []
    r = x
    for _ in range(parts):
        p = r.astype(BF16)
        out.append(p)
        r = r - p.astype(F32)
    return out


def _post_norm(x, out, gate, g, b, alpha):
    y = alpha * x + (1.0 + gate) * out
    mu = jnp.mean(y, axis=-1, keepdims=True)
    d = y - mu
    var = jnp.mean(d * d, axis=-1, keepdims=True)
    return d * lax.rsqrt(var + LN_EPS) * g + b


def _mod_kernel(c_ref, w_ref, b_ref, o_ref):
    c = c_ref[...]
    s = (c * _sigmoid(c)).astype(BF16)
    o_ref[0] = jnp.dot(s, w_ref[0].astype(BF16), preferred_element_type=F32) + b_ref[0]


def _mod_call(c_all, w_mod, b_mod):
    depth, d, d6 = w_mod.shape
    r = c_all.shape[0]
    tn = _row_tile(d6, 1536)
    return pl.pallas_call(
        _mod_kernel,
        grid=(depth, d6 // tn),
        in_specs=[pl.BlockSpec((r, d), lambda l, j: (0, 0)),
                  pl.BlockSpec((1, d, tn), lambda l, j: (l, 0, j)),
                  pl.BlockSpec((1, 1, tn), lambda l, j: (l, 0, j))],
        out_specs=pl.BlockSpec((1, r, tn), lambda l, j: (l, 0, j)),
        out_shape=jax.ShapeDtypeStruct((depth, r, d6), F32),
        name="adaln_mod",
        compiler_params=_cparams(("parallel", "parallel"), 3 * d * tn * 4),
    )(c_all, w_mod, b_mod.reshape(depth, 1, d6))


def _discretise(lr, li, ldt):
    dt = jnp.exp(ldt)
    mag = jnp.exp(lr * dt)
    a_re = mag * jnp.cos(li * dt)
    a_im = mag * jnp.sin(li * dt)
    den = lr * lr + li * li
    nr = a_re - 1.0
    k_re = (nr * lr + a_im * li) / den
    k_im = (a_im * lr - nr * li) / den
    return a_re, a_im, k_re, k_im


def _cmul(ar, ai, br, bi):
    return ar * br - ai * bi, ar * bi + ai * br


def _ssm_prep_kernel(lr_row, li_row, dt_row, lr_col, li_col, dt_col, b_re, b_im,
                     a_re_o, a_im_o, pneg_re_o, pneg_im_o, ppos_re_o, ppos_im_o, bb_re_o, bb_im_o):
    a_re, a_im, _, _ = _discretise(lr_row[...], li_row[...], dt_row[...])
    a_re_o[...] = a_re
    a_im_o[...] = a_im
    _, _, k_re, k_im = _discretise(lr_col[...], li_col[...], dt_col[...])
    br = b_re[...]
    bi = b_im[...]
    bb_re_o[...] = k_re * br - k_im * bi
    bb_im_o[...] = k_re * bi + k_im * br

    chunk = ppos_re_o.shape[0]
    t = lax.broadcasted_iota(jnp.int32, ppos_re_o.shape, 0)
    n2 = a_re * a_re + a_im * a_im
    for (sq_re, sq_im, o_re, o_im) in ((a_re, a_im, ppos_re_o, ppos_im_o),
                                       (a_re / n2, -a_im / n2, pneg_re_o, pneg_im_o)):
        p_re = jnp.ones(ppos_re_o.shape, F32)
        p_im = jnp.zeros(ppos_re_o.shape, F32)
        bit = 1
        while bit < chunk:
            on = (t & bit) != 0
            f_re = jnp.where(on, sq_re, 1.0)
            f_im = jnp.where(on, sq_im, 0.0)
            p_re, p_im = _cmul(p_re, p_im, f_re, f_im)
            sq_re, sq_im = _cmul(sq_re, sq_im, sq_re, sq_im)
            bit *= 2
        o_re[...] = p_re
        o_im[...] = p_im


def _ssm_prep(lam_re, lam_im, log_dt, b_re, b_im, c_re, c_im):
    g, p = lam_re.shape
    gs = b_re.shape[-1]
    gp = g * p
    dt = jnp.broadcast_to(log_dt[:, None], (g, p))
    rows = [a.reshape(1, gp) for a in (lam_re, lam_im, dt)]
    cols = [a.reshape(gp, 1) for a in (lam_re, lam_im, dt)]
    full = lambda shape: pl.BlockSpec(shape, lambda: tuple(0 for _ in shape))
    out_shapes = ([jax.ShapeDtypeStruct((1, gp), F32)] * 2 + [jax.ShapeDtypeStruct((SSM_CHUNK, gp), F32)] * 4
                  + [jax.ShapeDtypeStruct((gp, gs), F32)] * 2)
    a_re, a_im, pneg_re, pneg_im, ppos_re, ppos_im, bb_re, bb_im = pl.pallas_call(
        _ssm_prep_kernel,
        in_specs=[full((1, gp))] * 3 + [full((gp, 1))] * 3 + [full((gp, gs))] * 2,
        out_specs=[full(s.shape) for s in out_shapes],
        out_shape=out_shapes,
        name="ssm_prep",
        compiler_params=_cparams(None, 48 * SSM_CHUNK * gp * 4),
    )(*rows, *cols, b_re.reshape(gp, gs), b_im.reshape(gp, gs))

    gpb = V7X_MXU_DIM // gs
    nb = g // gpb
    eye = jnp.eye(gpb, dtype=F32)

    def in_proj(bb):
        t = bb.reshape(nb, gpb, p, gs).transpose(0, 1, 3, 2)
        return jnp.einsum("nghp,gk->nghkp", t, eye).reshape(nb, gpb * gs, gpb * p).astype(BF16)

    def out_proj(c):
        t = c.reshape(nb, gpb, gs, p)
        return jnp.einsum("nghp,gk->nkpgh", t, eye).reshape(nb, gpb * p, gpb * gs).astype(BF16)

    return dict(a_re=a_re, a_im=a_im, pneg_re=pneg_re, pneg_im=pneg_im, ppos_re=ppos_re, ppos_im=ppos_im,
                bb_re=in_proj(bb_re), bb_im=in_proj(bb_im), c_re=out_proj(c_re), c_im=out_proj(-c_im))


def _gelu_tanh(y):
    return 0.5 * y * (1.0 + jnp.tanh(0.7978845608028654 * (y + 0.044715 * (y * y * y))))


def _ssm_seq_kernel(x_ref, mod_ref, h0r_ref, h0i_ref, ar_ref, ai_ref, pnr_ref, pni_ref, ppr_ref, ppi_ref,
                    bbr_ref, bbi_ref, cr_ref, ci_ref, d_ref, tri_ref, z_ref, hr_ref, hi_ref):
    step = pl.program_id(1)

    @pl.when(step == 0)
    def _():
        hr_ref[0] = h0r_ref[0]
        hi_ref[0] = h0i_ref[0]

    chunk = x_ref.shape[1]
    u = x_ref[0] * (1.0 + mod_ref[0, 1]) + mod_ref[0, 0]
    ub = u.astype(BF16)
    nb, kin, kst = bbr_ref.shape
    sub = V7X_SUBLANES
    row0 = lax.broadcasted_iota(jnp.int32, (sub, kst), 0) == 0
    tri = tri_ref[...]
    cols = [slice(n * kst, (n + 1) * kst) for n in range(nb)]
    carry = [_cmul(ar_ref[:, cs], ai_ref[:, cs], hr_ref[0, :, cs], hi_ref[0, :, cs]) for cs in cols]
    xs = [(jnp.dot(ub[:, n * kin:(n + 1) * kin], bbr_ref[n], preferred_element_type=F32),
           jnp.dot(ub[:, n * kin:(n + 1) * kin], bbi_ref[n], preferred_element_type=F32)) for n in range(nb)]
    cums = []
    for cs, (x_re, x_im), (c_re, c_im) in zip(cols, xs, carry):
        x_re = jnp.concatenate([x_re[:sub] + jnp.where(row0, c_re, 0.0), x_re[sub:]], axis=0)
        x_im = jnp.concatenate([x_im[:sub] + jnp.where(row0, c_im, 0.0), x_im[sub:]], axis=0)
        scaled = _cmul(x_re, x_im, pnr_ref[:, cs], pni_ref[:, cs])
        cums.append([jnp.dot(tri, jnp.concatenate(_split_bf16(s, 2), axis=0), preferred_element_type=F32)
                     for s in scaled])
    ys = []
    for n, (cs, cum) in enumerate(zip(cols, cums)):
        h_re, h_im = _cmul(cum[0], cum[1], ppr_ref[:, cs], ppi_ref[:, cs])
        hr_ref[0, :, cs] = h_re[chunk - 1:chunk]
        hi_ref[0, :, cs] = h_im[chunk - 1:chunk]
        ys.append(jnp.dot(h_re.astype(BF16), cr_ref[n], preferred_element_type=F32)
                  + jnp.dot(h_im.astype(BF16), ci_ref[n], preferred_element_type=F32))
    y = jnp.concatenate(ys, axis=1) + u * d_ref[...]
    z_ref[0] = _gelu_tanh(y).astype(BF16)


def _ssm_seq_call(x, mod, h0_re, h0_im, sp, d_skip):
    n, t, d = x.shape
    gp = sp["a_re"].shape[1]
    chunk = SSM_CHUNK
    assert t % chunk == 0
    tri = jnp.tril(jnp.ones((chunk, chunk), F32)).astype(BF16)
    tri2 = jnp.concatenate([tri, tri], axis=1)
    const = lambda a: pl.BlockSpec(a.shape, lambda i, j: tuple(0 for _ in a.shape))
    consts = [sp["a_re"], sp["a_im"], sp["pneg_re"], sp["pneg_im"], sp["ppos_re"], sp["ppos_im"],
              sp["bb_re"], sp["bb_im"], sp["c_re"], sp["c_im"], d_skip.reshape(1, d), tri2]
    state = pl.BlockSpec((1, 1, gp), lambda i, j: (i, 0, 0))
    const_bytes = sum(a.size * a.dtype.itemsize for a in consts)
    z, h_re, h_im = pl.pallas_call(
        _ssm_seq_kernel,
        grid=(n, t // chunk),
        in_specs=[pl.BlockSpec((1, chunk, d), lambda i, j: (i, j, 0)),
                  pl.BlockSpec((1, 6, 1, d), lambda i, j: (i, 0, 0, 0)), state, state]
                 + [const(a) for a in consts],
        out_specs=[pl.BlockSpec((1, chunk, d), lambda i, j: (i, j, 0)), state, state],
        out_shape=[jax.ShapeDtypeStruct((n, t, d), BF16),
                   jax.ShapeDtypeStruct((n, 1, gp), F32), jax.ShapeDtypeStruct((n, 1, gp), F32)],
        name="ssm_seq",
        compiler_params=_cparams(("parallel", "arbitrary"), 2 * const_bytes + 40 * chunk * gp),
    )(x, mod, h0_re.reshape(n, 1, gp), h0_im.reshape(n, 1, gp), *consts)
    return z, h_re.reshape(n, gp), h_im.reshape(n, gp)


def _ssm_step_kernel(x_ref, mod_ref, h0r_ref, h0i_ref, ar_ref, ai_ref, bbr_ref, bbi_ref, cr_ref, ci_ref, d_ref,
                     z_ref, hr_ref, hi_ref):
    u = x_ref[0] * (1.0 + mod_ref[0, 1]) + mod_ref[0, 0]
    ub = u.astype(BF16)
    nb, kin, kst = bbr_ref.shape
    ys = []
    for n in range(nb):
        cs = slice(n * kst, (n + 1) * kst)
        ubn = ub[:, n * kin:(n + 1) * kin]
        c_re, c_im = _cmul(ar_ref[:, cs], ai_ref[:, cs], h0r_ref[:, cs], h0i_ref[:, cs])
        h_re = jnp.dot(ubn, bbr_ref[n], preferred_element_type=F32) + c_re
        h_im = jnp.dot(ubn, bbi_ref[n], preferred_element_type=F32) + c_im
        hr_ref[:, cs] = h_re
        hi_ref[:, cs] = h_im
        ys.append(jnp.dot(h_re.astype(BF16), cr_ref[n], preferred_element_type=F32)
                  + jnp.dot(h_im.astype(BF16), ci_ref[n], preferred_element_type=F32))
    y = jnp.concatenate(ys, axis=1) + u * d_ref[...]
    z_ref[0] = _gelu_tanh(y).astype(BF16)


def _ssm_step_call(x, mod, h0_re, h0_im, sp, d_skip):
    _, rows, d = x.shape
    gp = sp["a_re"].shape[1]
    args = [x, mod, h0_re, h0_im, sp["a_re"], sp["a_im"], sp["bb_re"], sp["bb_im"], sp["c_re"], sp["c_im"],
            d_skip.reshape(1, d)]
    full = lambda a: pl.BlockSpec(a.shape, lambda: tuple(0 for _ in a.shape))
    out_shapes = [jax.ShapeDtypeStruct((1, rows, d), BF16),
                  jax.ShapeDtypeStruct((rows, gp), F32), jax.ShapeDtypeStruct((rows, gp), F32)]
    return pl.pallas_call(
        _ssm_step_kernel,
        in_specs=[full(a) for a in args],
        out_specs=[full(s) for s in out_shapes],
        out_shape=out_shapes,
        name="ssm_step",
        compiler_params=_cparams(None, 4 * sum(a.size * a.dtype.itemsize for a in args)),
    )(*args)


def _mixer_post_kernel(a_ref, x_ref, mod_ref, w_ref, lng_ref, lnb_ref, *rest, glu, router, alpha):
    if router:
        wr_ref, br_ref, x1_ref, h2_ref, lg_ref = rest
    else:
        x1_ref, h2_ref = rest
    proj = jnp.dot(a_ref[0], w_ref[...], preferred_element_type=F32)
    if glu:
        d = proj.shape[1] // 2
        proj = proj[:, :d] * _sigmoid(proj[:, d:])
    x1 = _post_norm(x_ref[0], proj, mod_ref[0, 2], lng_ref[...], lnb_ref[...], alpha)
    x1_ref[0] = x1
    h2 = x1 * (1.0 + mod_ref[0, 4]) + mod_ref[0, 3]
    h2_ref[0] = h2.astype(BF16)
    if router:
        nt = (((1,), (1,)), ((), ()))
        h_hi, h_lo = _split_bf16(h2, 2)
        w_hi, w_lo = _split_bf16(wr_ref[...], 2)
        lg = (lax.dot_general(w_hi, h_hi, nt, preferred_element_type=F32)
              + lax.dot_general(w_lo, h_hi, nt, preferred_element_type=F32)
              + lax.dot_general(w_hi, h_lo, nt, preferred_element_type=F32))
        lg_ref[...] = lg + br_ref[...]


def _mixer_post_call(a, x, mod, w, ln_g, ln_b, alpha, *, glu, w_router=None, b_router=None):
    n, t, d = x.shape
    tm = _row_tile(t, ROW_TILE)
    rmod = mod.shape[2]
    router = w_router is not None
    row = lambda width: pl.BlockSpec((1, tm, width), lambda i, j: (i, j, 0))
    const = lambda arr: pl.BlockSpec(arr.shape, lambda i, j: tuple(0 for _ in arr.shape))
    mod_spec = pl.BlockSpec((1, 6, rmod, d), (lambda i, j: (i, 0, 0, 0)) if rmod == 1 else (lambda i, j: (i, 0, j, 0)))
    args = [a, x, mod, w, ln_g.reshape(1, d), ln_b.reshape(1, d)]
    in_specs = [row(a.shape[2]), row(d), mod_spec, const(w), const(args[4]), const(args[5])]
    out_specs = [row(d), row(d)]
    out_shape = [jax.ShapeDtypeStruct((n, t, d), F32), jax.ShapeDtypeStruct((n, t, d), BF16)]
    if router:
        e = w_router.shape[1]
        args += [w_router.T, b_router.reshape(e, 1)]
        in_specs += [const(args[-2]), const(args[-1])]
        out_specs.append(pl.BlockSpec((e, tm), lambda i, j: (0, i * (t // tm) + j)))
        out_shape.append(jax.ShapeDtypeStruct((e, n * t), F32))
    vmem = 2 * w.size * 2 + 12 * tm * w.shape[1] * 4 + 8 * tm * d * 4
    return pl.pallas_call(
        functools.partial(_mixer_post_kernel, glu=glu, router=router, alpha=alpha),
        grid=(n, t // tm), in_specs=in_specs, out_specs=out_specs, out_shape=out_shape,
        name="mixer_post",
        compiler_params=_cparams(("parallel", "parallel"), vmem),
    )(*args)


def _swiglu_chunks(h, wg_ref, wu_ref, wd_ref, lead, fc):
    f = wg_ref.shape[-1]
    acc = None
    for lo in range(0, f, fc):
        hi = min(lo + fc, f)
        g = jnp.dot(h, wg_ref[lead + (slice(None), slice(lo, hi))], preferred_element_type=F32)
        u = jnp.dot(h, wu_ref[lead + (slice(None), slice(lo, hi))], preferred_element_type=F32)
        a = (g * _sigmoid(g) * u).astype(BF16)
        part = jnp.dot(a, wd_ref[lead + (slice(lo, hi), slice(None))], preferred_element_type=F32)
        acc = part if acc is None else acc + part
    return acc


def _mixer_ffn_kernel(a_ref, x_ref, mod_ref, w_ref, lng1_ref, lnb1_ref, wg_ref, wu_ref, wd_ref, lng2_ref, lnb2_ref,
                      o_ref, *, glu, alpha, fc):
    proj = jnp.dot(a_ref[0], w_ref[...], preferred_element_type=F32)
    if glu:
        d = proj.shape[1] // 2
        proj = proj[:, :d] * _sigmoid(proj[:, d:])
    x1 = _post_norm(x_ref[0], proj, mod_ref[0, 2], lng1_ref[...], lnb1_ref[...], alpha)
    h2 = (x1 * (1.0 + mod_ref[0, 4]) + mod_ref[0, 3]).astype(BF16)
    out = _swiglu_chunks(h2, wg_ref, wu_ref, wd_ref, (), fc)
    o_ref[0] = _post_norm(x1, out, mod_ref[0, 5], lng2_ref[...], lnb2_ref[...], alpha)


def _mixer_ffn_call(a, x, mod, w, ln_g1, ln_b1, wg, wu, wd, ln_g2, ln_b2, alpha, *, glu):
    n, t, d = x.shape
    f = wg.shape[1]
    tm = _row_tile(t, ROW_TILE)
    rmod = mod.shape[2]
    row = lambda width: pl.BlockSpec((1, tm, width), lambda i, j: (i, j, 0))
    const = lambda arr: pl.BlockSpec(arr.shape, lambda i, j: tuple(0 for _ in arr.shape),
                                     pipeline_mode=pl.Buffered(1))
    mod_spec = pl.BlockSpec((1, 6, rmod, d), (lambda i, j: (i, 0, 0, 0)) if rmod == 1 else (lambda i, j: (i, 0, j, 0)))
    fc = 2 * V7X_MXU_DIM
    consts = [w, ln_g1.reshape(1, d), ln_b1.reshape(1, d), wg, wu, wd, ln_g2.reshape(1, d), ln_b2.reshape(1, d)]
    vmem = (3 * d * f + w.size) * 2 + 12 * tm * w.shape[1] * 4 + 12 * tm * d * 4 + 6 * tm * fc * 4
    return pl.pallas_call(
        functools.partial(_mixer_ffn_kernel, glu=glu, alpha=alpha, fc=fc),
        grid=(n, t // tm),
        in_specs=[row(a.shape[2]), row(d), mod_spec] + [const(c) for c in consts],
        out_specs=row(d),
        out_shape=jax.ShapeDtypeStruct((n, t, d), F32), name="mixer_ffn",
        compiler_params=_cparams(("parallel", "parallel"), vmem),
    )(a, x, mod, *consts)


MOE_DISPATCH_TILE = 512
MOE_ROW_TILE = 512
MOE_COMBINE_TILE = 128
BF16_ROWS = 16


def _route_kernel(lg_ref, upper_ref, gate_ref, sel_ref, rank_ref, cnt_ref):
    @pl.when(pl.program_id(0) == 0)
    def _():
        cnt_ref[...] = jnp.zeros_like(cnt_ref)

    lg = lg_ref[...]
    e = lg.shape[0]
    idx = lax.broadcasted_iota(jnp.int32, lg.shape, 0)
    m1 = jnp.max(lg, axis=0, keepdims=True)
    i1 = jnp.min(jnp.where(lg == m1, idx, e), axis=0, keepdims=True)
    rest = jnp.where(idx == i1, -jnp.inf, lg)
    m2 = jnp.max(rest, axis=0, keepdims=True)
    i2 = jnp.min(jnp.where(rest == m2, idx, e), axis=0, keepdims=True)
    e2 = jnp.exp(m2 - m1)
    den = 1.0 + e2
    gate_ref[...] = jnp.where(idx == i1, 1.0 / den, 0.0) + jnp.where(idx == i2, e2 / den, 0.0)
    sel = jnp.where(idx == i1, 1.0, 0.0) + jnp.where(idx == i2, 1.0, 0.0)
    sel_ref[...] = sel
    rank_ref[...] = cnt_ref[...] + jnp.dot(sel.astype(BF16), upper_ref[...], preferred_element_type=F32)
    cnt_ref[...] += jnp.sum(sel, axis=1, keepdims=True)


def _route_call(logits_t):
    e, m = logits_t.shape
    tr = _row_tile(m, ROW_TILE)
    upper = jnp.triu(jnp.ones((tr, tr), F32), k=1).astype(BF16)
    blk = pl.BlockSpec((e, tr), lambda i: (0, i))
    out_shape = [jax.ShapeDtypeStruct((e, m), F32)] * 3 + [jax.ShapeDtypeStruct((e, 1), F32)]
    return pl.pallas_call(
        _route_kernel, grid=(m // tr,),
        in_specs=[blk, pl.BlockSpec((tr, tr), lambda i: (0, 0))],
        out_specs=[blk, blk, blk, pl.BlockSpec((e, 1), lambda i: (0, 0))],
        out_shape=out_shape, name="moe_route",
        compiler_params=_cparams(("arbitrary",), 4 * tr * tr * 2 + 64 * e * tr * 4),
    )(logits_t, upper)


def _moe_schedule(gate_t, sel_t, rank_t, cnt, tt, tme, tc):
    e, m = sel_t.shape
    cnt_i = cnt[:, 0].astype(jnp.int32)
    seg_tiles = cnt_i // tme + 1
    seg_end = jnp.cumsum(seg_tiles)
    seg_start = seg_end - seg_tiles
    offsets = seg_start * tme
    n_tiles = (TOP_K * m) // tme + e
    tile_ids = jnp.arange(n_tiles, dtype=jnp.int32)
    tile_expert = jnp.minimum(jnp.sum(tile_ids[:, None] >= seg_end[None, :], axis=1), e - 1).astype(jnp.int32)
    tile_valid = (tile_ids < seg_end[-1]).astype(jnp.int32)
    rank_i = rank_t.astype(jnp.int32)
    rank_at = jnp.concatenate([rank_i[:, ::tt], cnt_i[:, None]], axis=1)
    done_tiles = rank_at // tme
    w0 = (rank_at - done_tiles * tme) // BF16_ROWS * BF16_ROWS
    completes = jnp.concatenate([rank_at[:, 1:] >= (done_tiles[:, :-1] + 1) * tme,
                                 jnp.ones((e, 1), bool)], axis=1).astype(jnp.int32)
    fits = rank_at[:, 1:] - rank_at[:, :-1] + BF16_ROWS <= _dispatch_narrow_window(tt)
    completes = completes + 2 * jnp.concatenate([fits, jnp.zeros((e, 1), bool)], axis=1).astype(jnp.int32)
    pos = jnp.where(sel_t > 0, offsets[:, None].astype(F32) + rank_t, -1.0).T
    blk0 = ((offsets[:, None] + rank_i[:, ::tc]) // tc).T
    return dict(base=(done_tiles * tme + w0).reshape(-1), w0=w0.reshape(-1), completes=completes.reshape(-1),
                out_tile=(seg_start[:, None] + done_tiles).reshape(-1), tile_expert=tile_expert,
                tile_valid=tile_valid, n_tiles=n_tiles, blk0=blk0.reshape(-1).astype(jnp.int32), pos=pos,
                gates=gate_t.T)


def _dispatch_narrow_window(tt):
    half = tt // 2
    return half if half % BF16_ROWS == 0 and half > BF16_ROWS else tt + BF16_ROWS


def _dispatch_kernel(base_ref, w0_ref, done_ref, tile_ref, h_ref, sel_ref, rank_ref, o_ref, stage_ref, *, tme):
    del tile_ref
    ex = pl.program_id(0)
    j = pl.program_id(1)
    steps = pl.num_programs(1)
    flat = ex * steps + j
    tt = h_ref.shape[0]
    win = tt + BF16_ROWS

    @pl.when(j == 0)
    def _():
        stage_ref[...] = jnp.zeros_like(stage_ref)

    def place(rows_in_window):
        w0 = pl.multiple_of(w0_ref[flat], BF16_ROWS)
        local = rank_ref[pl.ds(ex, 1), :] - base_ref[flat].astype(F32)
        local = jnp.where(sel_ref[pl.ds(ex, 1), :] > 0, local, -1.0)
        rows = lax.broadcasted_iota(jnp.int32, (rows_in_window, tt), 0).astype(F32)
        onehot = jnp.where(rows == local, 1.0, 0.0).astype(BF16)
        stage_ref[pl.ds(w0, rows_in_window), :] += jnp.dot(onehot, h_ref[...], preferred_element_type=F32)

    narrow = _dispatch_narrow_window(tt)
    flag = done_ref[flat]

    @pl.when((j < steps - 1) & (flag >= 2))
    def _():
        place(narrow)

    @pl.when((j < steps - 1) & (flag < 2))
    def _():
        place(win)

    @pl.when(flag % 2 == 1)
    def _():
        o_ref[...] = stage_ref[0:tme, :].astype(BF16)
        tail = stage_ref[tme:, :]
        stage_ref[...] = jnp.zeros_like(stage_ref)
        stage_ref[0:win, :] = tail


def _dispatch_call(h, sel_t, rank_t, sched, tt, tme):
    m, d = h.shape
    e = sel_t.shape[0]
    nj = m // tt
    steps = nj + 1
    clamp = lambda j: jnp.minimum(j, nj - 1)
    grid_spec = pltpu.PrefetchScalarGridSpec(
        num_scalar_prefetch=4, grid=(e, steps),
        in_specs=[pl.BlockSpec((tt, d), lambda ex, j, *_: (clamp(j), 0)),
                  pl.BlockSpec((e, tt), lambda ex, j, *_: (0, clamp(j))),
                  pl.BlockSpec((e, tt), lambda ex, j, *_: (0, clamp(j)))],
        out_specs=pl.BlockSpec((tme, d), lambda ex, j, base, w0, done, tile: (tile[ex * steps + j], 0)),
        scratch_shapes=[pltpu.VMEM((tme + tt + BF16_ROWS, d), F32)])
    return pl.pallas_call(
        functools.partial(_dispatch_kernel, tme=tme), grid_spec=grid_spec,
        out_shape=jax.ShapeDtypeStruct((sched["n_tiles"] * tme, d), BF16), name="moe_dispatch",
        compiler_params=_cparams(("arbitrary", "arbitrary"), 12 * (tme + tt) * d * 4),
    )(sched["base"], sched["w0"], sched["completes"], sched["out_tile"], h, sel_t, rank_t)


def _group_ffn_kernel(texp_ref, valid_ref, x_ref, wg_ref, wu_ref, wd_ref, y_ref, *, fc):
    del texp_ref
    live = valid_ref[pl.program_id(0)] == 1

    @pl.when(live)
    def _():
        y_ref[...] = _swiglu_chunks(x_ref[...], wg_ref, wu_ref, wd_ref, (0, 0), fc).astype(BF16)

    @pl.when(jnp.logical_not(live))
    def _():
        y_ref[...] = jnp.zeros_like(y_ref)


def _group_ffn_call(xs, sched, wg, wu, wd, layer, tme):
    rows, d = xs.shape
    f = wg.shape[3]
    fc = 2 * V7X_MXU_DIM
    row = pl.BlockSpec((tme, d), lambda i, texp, valid: (i, 0))
    grid_spec = pltpu.PrefetchScalarGridSpec(
        num_scalar_prefetch=2, grid=(rows // tme,),
        in_specs=[row,
                  pl.BlockSpec((1, 1, d, f), lambda i, texp, valid: (layer, texp[i], 0, 0)),
                  pl.BlockSpec((1, 1, d, f), lambda i, texp, valid: (layer, texp[i], 0, 0)),
                  pl.BlockSpec((1, 1, f, d), lambda i, texp, valid: (layer, texp[i], 0, 0))],
        out_specs=row)
    return pl.pallas_call(
        functools.partial(_group_ffn_kernel, fc=fc), grid_spec=grid_spec,
        out_shape=jax.ShapeDtypeStruct((rows, d), BF16), name="moe_group_ffn",
        compiler_params=_cparams(("arbitrary",), 2 * 3 * d * f * 2 + 8 * tme * d * 4 + 6 * tme * fc * 4),
    )(sched["tile_expert"], sched["tile_valid"], xs, wg, wu, wd)


def _combine_post_kernel(blk_ref, x_ref, mod_ref, pos_ref, gate_ref, lng_ref, lnb_ref, *rest, experts, alpha):
    y_refs, o_ref = rest[:2 * experts], rest[2 * experts]
    tc = x_ref.shape[1]
    tile = pl.program_id(0) * pl.num_programs(1) + pl.program_id(1)
    lane = lax.broadcasted_iota(jnp.int32, (tc, 2 * tc), 1).astype(F32)
    pos = pos_ref[...]
    gates = gate_ref[...]
    acc = jnp.zeros(x_ref.shape[1:], F32)
    for e in range(experts):
        base = (blk_ref[tile * experts + e] * tc).astype(F32)
        onehot = jnp.where(lane == pos[:, e:e + 1] - base, 1.0, 0.0).astype(BF16)
        window = jnp.concatenate([y_refs[2 * e][...], y_refs[2 * e + 1][...]], axis=0)
        acc = acc + gates[:, e:e + 1] * jnp.dot(onehot, window, preferred_element_type=F32)
    o_ref[0] = _post_norm(x_ref[0], acc, mod_ref[0, 5], lng_ref[...], lnb_ref[...], alpha)


def _combine_post_call(ys, x, mod, sched, ln_g, ln_b, alpha, tc):
    n, t, d = x.shape
    e = sched["pos"].shape[1]
    nj = t // tc
    nblk = ys.shape[0] // tc
    rmod = mod.shape[2]
    row = pl.BlockSpec((1, tc, d), lambda i, j, blk: (i, j, 0))
    per_tok = pl.BlockSpec((tc, e), lambda i, j, blk: (i * nj + j, 0))
    const = lambda arr: pl.BlockSpec(arr.shape, lambda i, j, blk: tuple(0 for _ in arr.shape))
    mod_spec = pl.BlockSpec((1, 6, rmod, d),
                            (lambda i, j, blk: (i, 0, 0, 0)) if rmod == 1 else (lambda i, j, blk: (i, 0, j, 0)))

    def gather_spec(ex, half):
        return pl.BlockSpec((tc, d), lambda i, j, blk: (jnp.minimum(blk[(i * nj + j) * e + ex] + half, nblk - 1), 0))

    lng, lnb = ln_g.reshape(1, d), ln_b.reshape(1, d)
    grid_spec = pltpu.PrefetchScalarGridSpec(
        num_scalar_prefetch=1, grid=(n, nj),
        in_specs=[row, mod_spec, per_tok, per_tok, const(lng), const(lnb)]
                 + [gather_spec(ex, half) for ex in range(e) for half in range(2)],
        out_specs=row)
    return pl.pallas_call(
        functools.partial(_combine_post_kernel, experts=e, alpha=alpha), grid_spec=grid_spec,
        out_shape=jax.ShapeDtypeStruct((n, t, d), F32), name="moe_combine_post",
        compiler_params=_cparams(("parallel", "parallel"), 4 * 2 * e * tc * d * 2 + 16 * tc * d * 4),
    )(sched["blk0"], x, mod, sched["pos"], sched["gates"], lng, lnb, *([ys] * (2 * e)))


def _moe_call(h2, logits_t, x1, mod, wg, wu, wd, layer, ln_g, ln_b, alpha):
    n, t, d = x1.shape
    m = n * t
    tt = min(MOE_DISPATCH_TILE, m)
    tme = min(MOE_ROW_TILE, m)
    tc = min(MOE_COMBINE_TILE, t)
    gate_t, sel_t, rank_t, cnt = _route_call(logits_t)
    sched = _moe_schedule(gate_t, sel_t, rank_t, cnt, tt, tme, tc)
    xs = _dispatch_call(h2.reshape(m, d), sel_t, rank_t, sched, tt, tme)
    ys = _group_ffn_call(xs, sched, wg, wu, wd, layer, tme)
    return _combine_post_call(ys, x1, mod, sched, ln_g, ln_b, alpha, tc)


def _log_sigmoid(z):
    return -(jnp.maximum(-z, 0.0) + jnp.log1p(jnp.exp(-jnp.abs(z))))


ATTN_LANES = V7X_LANES
AUG_TERMS = 3
LOG2E = 1.4426950408889634


def _aug_selectors(heads, hd, offset):
    rows = jnp.arange(AUG_TERMS * heads)
    col = (rows % heads) * ATTN_LANES + hd + offset + rows // heads
    return jnp.zeros((AUG_TERMS * heads, heads * ATTN_LANES), F32).at[rows, col].set(1.0).astype(BF16)


def _aug_ones(heads, hd, offset, count):
    lane = jnp.arange(heads * ATTN_LANES) % ATTN_LANES
    return ((lane >= hd + offset) & (lane < hd + offset + count)).astype(F32).reshape(1, heads * ATTN_LANES)


def _place_terms(x, sel_ref, sign):
    terms = jnp.concatenate([t.astype(F32) for t in _split_bf16(x, AUG_TERMS)], axis=1).astype(BF16)
    return sign * jnp.dot(terms, sel_ref[...], preferred_element_type=F32)


def _store_head_operands(dense, tail, out_ref, hd):
    per_tile = ATTN_LANES // hd
    own = lax.broadcasted_iota(jnp.int32, (dense.shape[0], ATTN_LANES), 1) < hd
    for h in range(out_ref.shape[1]):
        tile = dense[:, (h // per_tile) * ATTN_LANES:(h // per_tile + 1) * ATTN_LANES]
        if h % per_tile:
            tile = pltpu.roll(tile, ATTN_LANES - (h % per_tile) * hd, axis=1)
        out_ref[0, h] = jnp.where(own, tile, tail[:, h * ATTN_LANES:(h + 1) * ATTN_LANES]).astype(BF16)


def _kv_kernel(x_ref, wk_ref, wv_ref, wf_ref, bf_ref, *rest, cumulative):
    if cumulative:
        tri_ref, sel_ref, kone_ref, vone_ref, k_ref, v_ref, lf_ref, fk_ref, ka_ref, va_ref, carry_ref = rest
    else:
        k_ref, v_ref, lf_ref = rest
    xb = x_ref[0].astype(BF16)
    k = jnp.dot(xb, wk_ref[...], preferred_element_type=F32)
    v = jnp.dot(xb, wv_ref[...], preferred_element_type=F32)
    k_ref[0] = k
    v_ref[0] = v
    lf = _log_sigmoid(jnp.dot(xb, wf_ref[...], preferred_element_type=F32) + bf_ref[...])
    lf_ref[0] = lf
    if cumulative:
        @pl.when(pl.program_id(1) == 0)
        def _():
            carry_ref[...] = jnp.zeros_like(carry_ref)

        fk = carry_ref[...] + jnp.dot(tri_ref[...], jnp.concatenate(_split_bf16(lf, 3), axis=0),
                                      preferred_element_type=F32)
        fk_ref[0] = fk
        carry_ref[...] = fk[fk.shape[0] - 1:]
        hd = k.shape[1] // ka_ref.shape[1]
        _store_head_operands(k, kone_ref[...] + _place_terms(fk * LOG2E, sel_ref, -1.0), ka_ref, hd)
        _store_head_operands(v, jnp.broadcast_to(vone_ref[...], (v.shape[0], vone_ref.shape[1])), va_ref, hd)


def _kv_call(x, wk, wv, wf, bf, *, cumulative):
    n, t, d = x.shape
    hh = wf.shape[1]
    hd = d // hh
    tm = _row_tile(t, ROW_TILE)
    row = lambda width: pl.BlockSpec((1, tm, width), lambda i, j: (i, j, 0))
    const = lambda arr: pl.BlockSpec(arr.shape, lambda i, j: tuple(0 for _ in arr.shape))
    args = [x, wk, wv, wf, bf.reshape(1, hh)]
    out_specs = [row(d), row(d), row(hh)]
    out_shape = [jax.ShapeDtypeStruct((n, t, d), F32)] * 2 + [jax.ShapeDtypeStruct((n, t, hh), F32)]
    scratch = []
    if cumulative:
        tri = jnp.tril(jnp.ones((tm, tm), F32)).astype(BF16)
        args += [jnp.concatenate([tri] * 3, axis=1),
                 _aug_selectors(hh, hd, AUG_TERMS), _aug_ones(hh, hd, 0, AUG_TERMS), _aug_ones(hh, hd, 0, 1)]
        head_rows = pl.BlockSpec((1, hh, tm, ATTN_LANES), lambda i, j: (i, 0, j, 0))
        out_specs += [row(hh), head_rows, head_rows]
        out_shape += [jax.ShapeDtypeStruct((n, t, hh), F32)] + [jax.ShapeDtypeStruct((n, hh, t, ATTN_LANES), BF16)] * 2
        scratch = [pltpu.VMEM((1, hh), F32)]
    return pl.pallas_call(
        functools.partial(_kv_kernel, cumulative=cumulative),
        grid=(n, t // tm),
        in_specs=[row(d)] + [const(a) for a in args[1:]],
        out_specs=out_specs, out_shape=out_shape, scratch_shapes=scratch, name="kv_proj",
        compiler_params=_cparams(("parallel", "arbitrary"), 16 * d * d + 40 * tm * d * 4),
    )(*args)


def _q_kernel(x_ref, mod_ref, w_ref, *rest, scale, augmented):
    h = (x_ref[0] * (1.0 + mod_ref[0, 1]) + mod_ref[0, 0]).astype(BF16)
    q = jnp.dot(h, w_ref[...], preferred_element_type=F32) * scale
    if augmented:
        fq_ref, sel_ref, one_ref, q_ref = rest
        _store_head_operands(q, one_ref[...] + _place_terms(fq_ref[0] * LOG2E, sel_ref, 1.0), q_ref,
                             q.shape[1] // q_ref.shape[1])
    else:
        q_ref, = rest
        q_ref[0] = q.astype(BF16)


def _q_call(x, mod, wq, heads, fq=None):
    n, t, d = x.shape
    hd = d // heads
    tm = _row_tile(t, ROW_TILE)
    rmod = mod.shape[2]
    row = lambda width: pl.BlockSpec((1, tm, width), lambda i, j: (i, j, 0))
    const = lambda arr: pl.BlockSpec(arr.shape, lambda i, j: tuple(0 for _ in arr.shape))
    mod_spec = pl.BlockSpec((1, 6, rmod, d), (lambda i, j: (i, 0, 0, 0)) if rmod == 1 else (lambda i, j: (i, 0, j, 0)))
    augmented = fq is not None
    if augmented:
        args = [x, mod, wq, fq, _aug_selectors(heads, hd, 0),
                _aug_ones(heads, hd, AUG_TERMS, AUG_TERMS)]
        in_specs = [row(d), mod_spec, const(args[2]), row(heads), const(args[4]), const(args[5])]
        out_specs = pl.BlockSpec((1, heads, tm, ATTN_LANES), lambda i, j: (i, 0, j, 0))
        out_shape = jax.ShapeDtypeStruct((n, heads, t, ATTN_LANES), BF16)
        scale = hd ** -0.5 * LOG2E
    else:
        args = [x, mod, wq]
        in_specs = [row(d), mod_spec, const(wq)]
        out_specs = row(d)
        out_shape = jax.ShapeDtypeStruct((n, t, d), BF16)
        scale = hd ** -0.5
    return pl.pallas_call(
        functools.partial(_q_kernel, scale=scale, augmented=augmented),
        grid=(n, t // tm), in_specs=in_specs, out_specs=out_specs, out_shape=out_shape, name="q_proj",
        compiler_params=_cparams(("parallel", "parallel"), 8 * d * d + 24 * tm * d * 4),
    )(*args)


ATTN_ROW_BLOCK = 256
ATTN_HEAD_GROUP = 16


def _attn_kernel(qt_ref, kt_ref, q_ref, k_ref, v_ref, o_ref, m_ref, acc_ref, *, hd, ratio):
    qi = qt_ref[pl.program_id(1)]
    kj = kt_ref[pl.program_id(1)]
    heads, tq = q_ref.shape[1], q_ref.shape[2]
    tk = k_ref.shape[2]
    lanes = acc_ref.shape[2]
    rb = min(ATTN_ROW_BLOCK, tq)
    group = math.gcd(heads, ATTN_HEAD_GROUP)
    nt = (((1,), (1,)), ((), ()))
    first_diag = qi * ratio

    @pl.when(kj == 0)
    def _():
        m_ref[...] = jnp.full_like(m_ref, NEG_INF)
        acc_ref[...] = jnp.zeros_like(acc_ref)

    def sweep(diagonal):
        blocks = [slice(r * rb, (r + 1) * rb) for r in range(tq // rb)]
        shift = (kj - first_diag) * tk

        def group_body(g, carry):
            hs = [g * group + i for i in range(group)]
            scores = [[lax.dot_general(q_ref[0, h, rows, :], k_ref[0, h], nt, preferred_element_type=F32)
                       for rows in blocks] for h in hs]
            for h, s_h in zip(hs, scores):
                vh = v_ref[0, h]
                m_all = m_ref[h]
                acc_all = acc_ref[h]
                m_out, acc_out = [], []
                for r, (rows, s) in enumerate(zip(blocks, s_h)):
                    if diagonal:
                        qpos = r * rb + lax.broadcasted_iota(jnp.int32, (rb, tk), 0)
                        kpos = shift + lax.broadcasted_iota(jnp.int32, (rb, tk), 1)
                        s = jnp.where(kpos <= qpos, s, NEG_INF)
                    m_prev = m_all[rows]
                    m_new = jnp.maximum(m_prev, jnp.max(s, axis=-1, keepdims=True))
                    p = jnp.exp2(s - jnp.concatenate([m_new] * (tk // lanes), axis=1))
                    acc_out.append(jnp.exp2(m_prev - m_new) * acc_all[rows]
                                   + jnp.dot(p.astype(BF16), vh, preferred_element_type=F32))
                    m_out.append(m_new)
                m_ref[h] = jnp.concatenate(m_out, axis=0)
                acc_ref[h] = jnp.concatenate(acc_out, axis=0)
            return carry
        lax.fori_loop(0, heads // group, group_body, 0)

    @pl.when(kj < first_diag)
    def _():
        sweep(False)

    @pl.when((kj >= first_diag) & (kj < first_diag + ratio))
    def _():
        sweep(True)

    @pl.when(kj == first_diag + ratio - 1)
    def _():
        per_tile = max(lanes // hd, 1)
        for h0 in range(0, heads, per_tile):
            outs = []
            for h in range(h0, min(h0 + per_tile, heads)):
                a = acc_ref[h]
                outs.append(a[:, :hd] / a[:, hd:hd + 1])
            o_ref[0, :, h0 * hd:(h0 + len(outs)) * hd] = jnp.concatenate(outs, axis=1).astype(BF16)


def _attn_call(qa, ka, va, hd):
    n, heads, t, lanes = qa.shape
    tk = _row_tile(t, ATTN_TILE)
    tq = _row_tile(t, ATTN_QUERY_TILE)
    ratio = tq // tk
    assert tq == ratio * tk and tk % lanes == 0
    pairs = [(a, b) for a in range(t // tq) for b in range((a + 1) * ratio)]
    q_tab = jnp.asarray([a for a, _ in pairs], jnp.int32)
    k_tab = jnp.asarray([b for _, b in pairs], jnp.int32)
    kv_spec = pl.BlockSpec((1, heads, tk, lanes), lambda i, s, qt, kt: (i, 0, kt[s], 0))
    group = math.gcd(heads, ATTN_HEAD_GROUP)
    grid_spec = pltpu.PrefetchScalarGridSpec(
        num_scalar_prefetch=2, grid=(n, len(pairs)),
        in_specs=[pl.BlockSpec((1, heads, tq, lanes), lambda i, s, qt, kt: (i, 0, qt[s], 0)), kv_spec, kv_spec],
        out_specs=pl.BlockSpec((1, tq, heads * hd), lambda i, s, qt, kt: (i, qt[s], 0)),
        scratch_shapes=[pltpu.VMEM((heads, tq, lanes), F32), pltpu.VMEM((heads, tq, lanes), F32)])
    return pl.pallas_call(
        functools.partial(_attn_kernel, hd=hd, ratio=ratio), grid_spec=grid_spec,
        out_shape=jax.ShapeDtypeStruct((n, t, heads * hd), BF16),
        name="causal_attn",
        compiler_params=_cparams(("parallel", "arbitrary"),
                                 heads * tq * lanes * (2 * 4 + 2 * 2) + 2 * 2 * heads * tk * lanes * 2
                                 + 2 * tq * heads * hd * 2 + 3 * group * tq * tk * 4),
    )(q_tab, k_tab, qa, ka, va)


def _column_to_row(col, eye):
    return jnp.sum(jnp.where(eye > 0, col, 0.0), axis=0, keepdims=True)


DECODE_PAGES_PER_STEP = 8


def _decode_attn_kernel(pt_ref, qb_ref, *rest, g):
    del pt_ref
    kt_refs, vt_refs, lft_refs = rest[:g], rest[g:2 * g], rest[2 * g:3 * g]
    (knew_ref, vnew_ref, lfnew_ref, later_ref, diag_ref, eye_ref, o_ref, m_ref, l_ref, acc_ref, suf_ref) = rest[3 * g:]
    step = pl.program_id(1)
    qb = qb_ref[0]
    eye = eye_ref[...]

    @pl.when(step == 0)
    def _():
        m_ref[...] = jnp.dot(qb, knew_ref[0].astype(BF16), preferred_element_type=F32)
        l_ref[...] = jnp.ones_like(l_ref)
        acc_ref[...] = jnp.broadcast_to(vnew_ref[0], acc_ref.shape)
        suf_ref[...] = lfnew_ref[0]

    scores = []
    after = suf_ref[...]
    for kt_ref, lft_ref in zip(kt_refs, lft_refs):
        lf = lft_ref[0]
        lf3 = jnp.concatenate(_split_bf16(lf, 3), axis=1)
        scores.append(jnp.dot(qb, kt_ref[0].astype(BF16), preferred_element_type=F32)
                      + jnp.dot(lf3, later_ref[...], preferred_element_type=F32) + after)
        after = after + jnp.sum(lf, axis=1, keepdims=True)
    suf_ref[...] = after
    m_prev = m_ref[...]
    m_new = m_prev
    for s in scores:
        m_new = jnp.maximum(m_new, jnp.max(s, axis=1, keepdims=True))
    alpha = jnp.exp(m_prev - m_new)
    l_new = alpha * l_ref[...]
    acc = _column_to_row(alpha, eye) * acc_ref[...]
    for s, vt_ref in zip(scores, vt_refs):
        p = jnp.exp(s - m_new)
        l_new = l_new + jnp.sum(p, axis=1, keepdims=True)
        acc = acc + lax.dot_general(vt_ref[0].astype(BF16), p.astype(BF16), (((1,), (1,)), ((), ())),
                                    preferred_element_type=F32)
    l_ref[...] = l_new
    acc_ref[...] = acc
    m_ref[...] = m_new

    @pl.when(step == pl.num_programs(1) - 1)
    def _():
        on_diag = diag_ref[...] > 0
        num = jnp.sum(jnp.where(on_diag, acc_ref[...], 0.0), axis=1, keepdims=True)
        den = jnp.sum(jnp.where(on_diag, _column_to_row(l_ref[...], eye), 0.0), axis=1, keepdims=True)
        o_ref[0] = (num / den).astype(BF16)


def _decode_attn_call(page_table, q, cache_k, cache_v, cache_logf, k_new, v_new, lf_new, heads):
    nseq, npages = page_table.shape
    n_phys, page, _, hd = cache_k.shape
    d = heads * hd
    seg = jnp.repeat(jnp.eye(heads, dtype=F32), hd, axis=1)
    qb = q[:, None, :] * seg[None].astype(BF16)
    later = jnp.tril(jnp.ones((page, page), F32), k=-1).astype(BF16)
    later3 = jnp.concatenate([later] * 3, axis=0)
    kt = cache_k.transpose(0, 2, 3, 1).reshape(n_phys, d, page)
    vt = cache_v.transpose(0, 2, 3, 1).reshape(n_phys, d, page)
    lft = cache_logf.transpose(0, 2, 1)
    g = math.gcd(npages, DECODE_PAGES_PER_STEP)

    def paged(rows):
        return [pl.BlockSpec((1, rows, page), functools.partial(
            lambda b, p, pt, i: (pt[b, npages - 1 - (p * g + i)], 0, 0), i=i)) for i in range(g)]

    per_seq = lambda arr: pl.BlockSpec((1,) + arr.shape[1:], lambda b, p, pt: (b,) + tuple(0 for _ in arr.shape[1:]))
    const = lambda arr: pl.BlockSpec(arr.shape, lambda b, p, pt: tuple(0 for _ in arr.shape))
    tail = [k_new.reshape(nseq, d, 1), v_new.reshape(nseq, d, 1), lf_new.reshape(nseq, heads, 1),
            later3, seg.T, jnp.eye(heads, dtype=F32)]
    args = [qb] + [kt] * g + [vt] * g + [lft] * g + tail
    grid_spec = pltpu.PrefetchScalarGridSpec(
        num_scalar_prefetch=1, grid=(nseq, npages // g),
        in_specs=[per_seq(qb)] + paged(d) + paged(d) + paged(heads)
                 + [per_seq(a) for a in tail[:3]] + [const(a) for a in tail[3:]],
        out_specs=pl.BlockSpec((1, d, 1), lambda b, p, pt: (b, 0, 0)),
        scratch_shapes=[pltpu.VMEM((heads, 1), F32), pltpu.VMEM((heads, 1), F32), pltpu.VMEM((d, heads), F32),
                        pltpu.VMEM((heads, 1), F32)])
    out = pl.pallas_call(
        functools.partial(_decode_attn_kernel, g=g), grid_spec=grid_spec,
        out_shape=jax.ShapeDtypeStruct((nseq, d, 1), BF16), name="paged_decode_attn",
        compiler_params=_cparams(("parallel", "arbitrary"), (8 + 7 * g) * page * d * 4),
    )(page_table, *args)
    return out.reshape(1, nseq, d)


def _trunk(x, mods, h0_re, h0_im, wts, ssm, *, sequence, paged=None):
    depth = mods.shape[0]
    n_a = wts["w_glu"].shape[0]
    heads = wts["w_f"].shape[1]
    nb, rows, d = x.shape
    hd = d // heads
    alpha = (2.0 * depth) ** 0.25
    new_re, new_im = [], []
    k_new = v_new = lf_new = None
    fk = k_aug = v_aug = None
    for l in range(depth):
        mod = mods[l]
        moe = l % 2 == 1
        li = l // 2
        router = dict(w_router=wts["w_router"][li], b_router=wts["b_router"][li]) if moe else {}
        if l < n_a:
            if sequence:
                z, hr, hi = _ssm_seq_call(x, mod, h0_re[l], h0_im[l], ssm[l], wts["ssm_d"][l])
            else:
                z, hr, hi = _ssm_step_call(x, mod, h0_re[l], h0_im[l], ssm[l], wts["ssm_d"][l])
            new_re.append(hr)
            new_im.append(hi)
            mixed, w_mix, glu = z, wts["w_glu"][l], True
        else:
            lb = l - n_a
            if sequence:
                o = _attn_call(_q_call(x, mod, wts["w_q"][lb], heads, fq=fk), k_aug, v_aug, hd)
            else:
                q = _q_call(x, mod, wts["w_q"][lb], heads)
                o = _decode_attn_call(paged[0], q[0], paged[1], paged[2], paged[3], k_new[0], v_new[0], lf_new[0],
                                      heads)
            mixed, w_mix, glu = o, wts["w_o"][lb], False
        if moe:
            x1, h2, logits_t = _mixer_post_call(mixed, x, mod, w_mix, wts["ln_g"][l, 0], wts["ln_b"][l, 0], alpha,
                                                glu=glu, **router)
            x = _moe_call(h2, logits_t, x1, mod, wts["w_exp_gate"], wts["w_exp_up"], wts["w_exp_down"], li,
                          wts["ln_g"][l, 1], wts["ln_b"][l, 1], alpha)
        else:
            x = _mixer_ffn_call(mixed, x, mod, w_mix, wts["ln_g"][l, 0], wts["ln_b"][l, 0], wts["w_ff_gate"][li],
                                wts["w_ff_up"][li], wts["w_ff_down"][li], wts["ln_g"][l, 1], wts["ln_b"][l, 1],
                                alpha, glu=glu)
        if l == n_a - 1:
            outs = _kv_call(x, wts["w_k"], wts["w_v"], wts["w_f"], wts["b_f"], cumulative=sequence)
            k_new, v_new, lf_new = outs[:3]
            if sequence:
                fk, k_aug, v_aug = outs[3:]
    return x, jnp.stack(new_re), jnp.stack(new_im), k_new, v_new, lf_new


def kernel(x_prompt, x_sample, c_prompt, c_sample, cache_k, cache_v, cache_logf, state_ssm_re, state_ssm_im,
           page_table, w_mod, b_mod, ln_g, ln_b, ssm_lam_re, ssm_lam_im, ssm_log_dt, ssm_b_re, ssm_b_im,
           ssm_c_re, ssm_c_im, ssm_d, w_glu, w_k, w_v, w_f, b_f, w_q, w_o, w_ff_gate, w_ff_up, w_ff_down,
           w_router, b_router, w_exp_gate, w_exp_up, w_exp_down):
    batch, seq, d = x_prompt.shape
    dec_batch, dec_seq, _ = x_sample.shape
    assert dec_seq == 1
    depth = w_mod.shape[0]
    n_a, g, p = ssm_lam_re.shape
    heads = w_f.shape[1]
    hd = d // heads
    gp = g * p

    n_c = batch + dec_batch
    c_all = jnp.concatenate([c_prompt, c_sample], axis=0)
    c_all = jnp.pad(c_all, ((0, (-n_c) % 8), (0, 0)))
    mods = _mod_call(c_all, w_mod, b_mod)
    mods_p = mods[:, :batch].reshape(depth, batch, 6, 1, d)
    mods_s = mods[:, batch:n_c].reshape(depth, dec_batch, 6, d).transpose(0, 2, 1, 3)[:, None]

    bf = lambda w: w.astype(BF16)
    wts = dict(ln_g=ln_g, ln_b=ln_b, ssm_d=ssm_d, w_glu=bf(w_glu), w_k=bf(w_k), w_v=bf(w_v), w_f=bf(w_f), b_f=b_f,
               w_q=bf(w_q), w_o=bf(w_o), w_ff_gate=bf(w_ff_gate), w_ff_up=bf(w_ff_up), w_ff_down=bf(w_ff_down),
               w_router=w_router, b_router=b_router, w_exp_gate=bf(w_exp_gate), w_exp_up=bf(w_exp_up),
               w_exp_down=bf(w_exp_down))
    ssm = [_ssm_prep(ssm_lam_re[l], ssm_lam_im[l], ssm_log_dt[l], ssm_b_re[l], ssm_b_im[l], ssm_c_re[l],
                     ssm_c_im[l]) for l in range(n_a)]

    h0 = jnp.zeros((n_a, batch, gp), F32)
    y_p, re_p, im_p, k_p, v_p, lf_p = _trunk(x_prompt, mods_p, h0, h0, wts, ssm, sequence=True)

    x_s = x_sample.reshape(1, dec_batch, d)
    y_s, re_s, im_s, k_s, v_s, lf_s = _trunk(
        x_s, mods_s, state_ssm_re.reshape(n_a, dec_batch, gp), state_ssm_im.reshape(n_a, dec_batch, gp), wts, ssm,
        sequence=False, paged=(page_table, cache_k, cache_v, cache_logf))

    return (y_p, y_s.reshape(dec_batch, 1, d),
            re_p.reshape(n_a, batch, g, p), im_p.reshape(n_a, batch, g, p),
            k_p.reshape(batch, seq, heads, hd), v_p.reshape(batch, seq, heads, hd), lf_p,
            re_s.reshape(n_a, dec_batch, g, p), im_s.reshape(n_a, dec_batch, g, p),
            k_s.reshape(dec_batch, 1, heads, hd), v_s.reshape(dec_batch, 1, heads, hd),
            lf_s.reshape(dec_batch, 1, heads))
```
